```python
import math
import jax, jax.numpy as jnp
from jax import lax
import numpy as np

D_MODEL = 2048
BATCH = 8
SEQ = 2048
DEPTH = 2

N_A_LAYERS = DEPTH // 2
N_B_LAYERS = DEPTH - N_A_LAYERS
HEAD_DIM_A = 128
N_HEADS_A = D_MODEL // HEAD_DIM_A
DILATION_PATTERNS = ((128, 1), (512, 4), (2048, 16))
N_GROUPS_A = len(DILATION_PATTERNS)
DIFF_HEAD_DIM = 128
N_HEADS_B = D_MODEL // (2 * DIFF_HEAD_DIM)
DIFF_QK_WIDTH = N_HEADS_B * 2 * DIFF_HEAD_DIM
DIFF_V_WIDTH = N_HEADS_B * 2 * DIFF_HEAD_DIM
D_FF = ((8 * D_MODEL // 3 + 127) // 128) * 128
MACARON_WEIGHT = 0.5
Q_BLOCK = 128
RMS_EPS = 1e-6
SUBLN_EPS = 1e-5

kernel_name = 'yoco_dilated_diff_macaron'


def rms_norm(x, gain, eps=RMS_EPS):
    xf = x.astype(jnp.float32)
    y = xf * lax.rsqrt(jnp.mean(xf * xf, axis=-1, keepdims=True) + eps)
    return (y * gain.astype(jnp.float32)).astype(x.dtype)


def swiglu(h, w_in, w_out):
    gate, up = jnp.split(h @ w_in, 2, axis=-1)
    return (jax.nn.silu(gate) * up) @ w_out


def alibi_slopes(n_heads):
    return 2.0 ** (-8.0 * jnp.arange(1, n_heads + 1, dtype=jnp.float32) / n_heads)


def diff_lambda_init(layer_idx):
    return 0.8 - 0.6 * math.exp(-0.3 * layer_idx)


def dilated_window_branch(q, k, v, slopes, window, dilation):
    b, s, h, dh = q.shape
    n_back = window // dilation
    sub_len = s // dilation
    n_blk = -(-sub_len // n_back)
    pad = n_blk * n_back - sub_len

    def strided(t):
        t = t.reshape(b, sub_len, dilation, h, dh).transpose(0, 2, 3, 1, 4)
        t = jnp.pad(t, ((0, 0), (0, 0), (0, 0), (0, pad), (0, 0)))
        return t.reshape(b, dilation, h, n_blk, n_back, dh)

    def with_prev_block(t):
        prev = jnp.pad(t, ((0, 0), (0, 0), (0, 0), (1, 0), (0, 0), (0, 0)))[:, :, :, :-1]
        return jnp.concatenate([prev, t], axis=4)

    qb = strided(q)
    kb = with_prev_block(strided(k))
    vb = with_prev_block(strided(v))
    scores = jnp.einsum('bchnqe,bchnke->bchnqk', qb, kb,
                        preferred_element_type=jnp.float32) * (dh ** -0.5)
    steps = jnp.arange(n_back)[:, None] + n_back - jnp.arange(2 * n_back)[None, :]
    key_idx = (jnp.arange(n_blk)[:, None, None] - 1) * n_back + jnp.arange(2 * n_back)[None, None, :]
    valid = (steps >= 0) & (steps <= n_back) & (key_idx >= 0)
    bias = -slopes[:, None, None, None] * (dilation * steps).astype(jnp.float32)
    scores = jnp.where(valid, scores + bias, -jnp.inf)
    lse = jax.nn.logsumexp(scores, axis=-1)
    probs = jnp.exp(scores - lse[..., None])
    out = jnp.einsum('bchnqk,bchnke->bchnqe', probs.astype(v.dtype), vb)
    out = out.reshape(b, dilation, h, n_blk * n_back, dh)[:, :, :, :sub_len]
    out = out.transpose(0, 3, 1, 2, 4).reshape(b, s, h, dh)
    lse = lse.reshape(b, dilation, h, n_blk * n_back)[..., :sub_len]
    lse = lse.transpose(0, 3, 1, 2).reshape(b, s, h)
    return out, lse


def dilated_attention_mixer(hn, w_qkv, w_o, slopes):
    b, s, _ = hn.shape
    qkv = (hn @ w_qkv).reshape(b, s, 3, N_GROUPS_A, N_HEADS_A, HEAD_DIM_A)
    outs, lses = [], []
    for g, (window, dilation) in enumerate(DILATION_PATTERNS):
        o, l = dilated_window_branch(qkv[:, :, 0, g], qkv[:, :, 1, g], qkv[:, :, 2, g],
                                     slopes, window, dilation)
        outs.append(o)
        lses.append(l)
    weights = jax.nn.softmax(jnp.stack(lses, axis=0), axis=0)
    merged = jnp.einsum('gbsh,gbshe->bshe', weights, jnp.stack(outs, axis=0).astype(jnp.float32))
    return merged.reshape(b, s, N_HEADS_A * HEAD_DIM_A).astype(hn.dtype) @ w_o


def shared_kv(x, gain, w_kv):
    b, s, _ = x.shape
    kv = rms_norm(x, gain) @ w_kv
    k = kv[..., :DIFF_QK_WIDTH].reshape(b, s, N_HEADS_B, 2, DIFF_HEAD_DIM)
    v = kv[..., DIFF_QK_WIDTH:].reshape(b, s, N_HEADS_B, 2 * DIFF_HEAD_DIM)
    return k, v


def differential_attention_mixer(hn, k, v, w_q, lam, subln_gain, w_o, slopes, lambda_init):
    b, s, _ = hn.shape
    q = (hn @ w_q).reshape(b, s, N_HEADS_B, 2, DIFF_HEAD_DIM)
    lam_f = lam.astype(jnp.float32)
    lam_full = (jnp.exp(jnp.sum(lam_f[0] * lam_f[1])) - jnp.exp(jnp.sum(lam_f[2] * lam_f[3]))
                + lambda_init)
    scale = DIFF_HEAD_DIM ** -0.5
    outs = []
    for start in range(0, s, Q_BLOCK):
        end = start + Q_BLOCK
        sc = jnp.einsum('bqhmd,bkhmd->bhmqk', q[:, start:end], k[:, :end],
                        preferred_element_type=jnp.float32) * scale
        qpos = jnp.arange(start, end)[:, None]
        kpos = jnp.arange(end)[None, :]
        dist = (qpos - kpos).astype(jnp.float32)
        sc = jnp.where(qpos >= kpos, sc - slopes[:, None, None, None] * dist, -jnp.inf)
        p = jax.nn.softmax(sc, axis=-1)
        attn = p[:, :, 0] - lam_full * p[:, :, 1]
        outs.append(jnp.einsum('bhqk,bkhe->bqhe', attn.astype(v.dtype), v[:, :end]))
    o = jnp.concatenate(outs, axis=1)
    o = rms_norm(o, subln_gain, SUBLN_EPS) * (1.0 - lambda_init)
    return o.reshape(b, s, DIFF_V_WIDTH) @ w_o


def setup_inputs(seed: int = 0) -> dict:
    key = jax.random.key(seed)
    ks = jax.random.split(key, 13)
    f32 = jnp.float32

    def dense(k, shape, fan_in):
        return jax.random.normal(k, shape, f32) * fan_in ** -0.5

    def gain(k, shape):
        return 1.0 + 0.02 * jax.random.normal(k, shape, f32)

    qkv_a_width = 3 * N_GROUPS_A * N_HEADS_A * HEAD_DIM_A
    return {
        'x': jax.random.normal(ks[0], (BATCH, SEQ, D_MODEL), f32),
        'norm_gains': gain(ks[1], (DEPTH, 3, D_MODEL)),
        'ffn_w_in': dense(ks[2], (DEPTH, 2, D_MODEL, 2 * D_FF), D_MODEL),
        'ffn_w_out': dense(ks[3], (DEPTH, 2, D_FF, D_MODEL), D_FF),
        'a_w_qkv': dense(ks[4], (N_A_LAYERS, D_MODEL, qkv_a_width), D_MODEL),
        'a_w_o': dense(ks[5], (N_A_LAYERS, N_HEADS_A * HEAD_DIM_A, D_MODEL), N_HEADS_A * HEAD_DIM_A),
        'kv_norm_gain': gain(ks[6], (D_MODEL,)),
        'b_w_kv': dense(ks[7], (D_MODEL, DIFF_QK_WIDTH + DIFF_V_WIDTH), D_MODEL),
        'b_w_q': dense(ks[8], (N_B_LAYERS, D_MODEL, DIFF_QK_WIDTH), D_MODEL),
        'b_lambda': 0.1 * jax.random.normal(ks[9], (N_B_LAYERS, 4, DIFF_HEAD_DIM), f32),
        'b_subln_gain': gain(ks[10], (N_B_LAYERS, 2 * DIFF_HEAD_DIM)),
        'b_w_o': dense(ks[11], (N_B_LAYERS, DIFF_V_WIDTH, D_MODEL), DIFF_V_WIDTH),
        'final_norm_gain': gain(ks[12], (D_MODEL,)),
    }


def reference(x, norm_gains, ffn_w_in, ffn_w_out, a_w_qkv, a_w_o, kv_norm_gain, b_w_kv,
              b_w_q, b_lambda, b_subln_gain, b_w_o, final_norm_gain):
    slopes_a = alibi_slopes(N_HEADS_A)
    slopes_b = alibi_slopes(N_HEADS_B)
    k_shared, v_shared = None, None
    for layer in range(DEPTH):
        x = x + MACARON_WEIGHT * swiglu(rms_norm(x, norm_gains[layer, 0]),
                                        ffn_w_in[layer, 0], ffn_w_out[layer, 0])
        hn = rms_norm(x, norm_gains[layer, 1])
        if layer < N_A_LAYERS:
            x = x + dilated_attention_mixer(hn, a_w_qkv[layer], a_w_o[layer], slopes_a)
        else:
            j = layer - N_A_LAYERS
            x = x + differential_attention_mixer(hn, k_shared, v_shared, b_w_q[j], b_lambda[j],
                                                 b_subln_gain[j], b_w_o[j], slopes_b,
                                                 diff_lambda_init(layer))
        x = x + MACARON_WEIGHT * swiglu(rms_norm(x, norm_gains[layer, 2]),
                                        ffn_w_in[layer, 1], ffn_w_out[layer, 1])
        if layer == N_A_LAYERS - 1:
            k_shared, v_shared = shared_kv(x, kv_norm_gain, b_w_kv)
    return rms_norm(x, final_norm_gain)
```

```python
import functools
import math

import jax
import jax.numpy as jnp
from jax import lax
from jax.experimental import pallas as pl
from jax.experimental.pallas import tpu as pltpu

F32 = jnp.float32
BF16 = jnp.bfloat16

HEAD_DIM = 128
DILATION_PATTERNS = ((128, 1), (512, 4), (2048, 16))
N_BACK = 128
MACARON_WEIGHT = 0.5
RMS_EPS = 1e-6
SUBLN_EPS = 1e-5
MASK_VALUE = -1e30
V7X_VMEM_LIMIT_BYTES = 56 * 1024 * 1024


def _params(semantics):
    return pltpu.CompilerParams(dimension_semantics=semantics,
                                vmem_limit_bytes=V7X_VMEM_LIMIT_BYTES)


def _rms_normed(x, gain, eps):
    y = x * lax.rsqrt(jnp.mean(x * x, axis=-1, keepdims=True) + eps)
    return y * gain


def _dot(a, b):
    return jnp.dot(a, b, preferred_element_type=F32)


def _dot_nt(a, b):
    return lax.dot_general(a, b, (((1,), (1,)), ((), ())), preferred_element_type=F32)


def _ffn_body(x_ref, g_ref, wg_ref, wu_ref, wo_ref, fg_ref, o_ref, xn_ref, *, final_norm):
    f = pl.program_id(1)

    @pl.when(f == 0)
    def _():
        x = x_ref[...]
        xn_ref[...] = _rms_normed(x, g_ref[...], RMS_EPS).astype(BF16)
        o_ref[...] = x

    xn = xn_ref[...]
    gate = _dot(xn, wg_ref[...])
    up = _dot(xn, wu_ref[...])
    h = (gate * jax.nn.sigmoid(gate) * up * MACARON_WEIGHT).astype(BF16)
    o_ref[...] += _dot(h, wo_ref[...])

    if final_norm:
        @pl.when(f == pl.num_programs(1) - 1)
        def _():
            o_ref[...] = _rms_normed(o_ref[...], fg_ref[...], RMS_EPS)


def _ffn(x2d, gain, w_gate, w_up, w_out, final_gain, *, tm, tf):
    m, d = x2d.shape
    fp = w_gate.shape[1]
    final_norm = final_gain is not None
    fg = final_gain if final_norm else gain
    return pl.pallas_call(
        functools.partial(_ffn_body, final_norm=final_norm),
        grid=(m // tm, fp // tf),
        in_specs=[
            pl.BlockSpec((tm, d), lambda i, f: (i, 0)),
            pl.BlockSpec((1, d), lambda i, f: (0, 0)),
            pl.BlockSpec((d, tf), lambda i, f: (0, f)),
            pl.BlockSpec((d, tf), lambda i, f: (0, f)),
            pl.BlockSpec((tf, d), lambda i, f: (f, 0)),
            pl.BlockSpec((1, d), lambda i, f: (0, 0)),
        ],
        out_specs=pl.BlockSpec((tm, d), lambda i, f: (i, 0)),
        out_shape=jax.ShapeDtypeStruct((m, d), F32),
        scratch_shapes=[pltpu.VMEM((tm, d), BF16)],
        compiler_params=_params(("parallel", "arbitrary")),
        name="ffn",
    )(x2d, gain.reshape(1, d), w_gate, w_up, w_out, fg.reshape(1, d))


def _norm_matmul_body(x_ref, g_ref, w_ref, o_ref, xn_ref):
    @pl.when(pl.program_id(1) == 0)
    def _():
        xn_ref[...] = _rms_normed(x_ref[...], g_ref[...], RMS_EPS).astype(BF16)

    o_ref[...] = _dot(xn_ref[...], w_ref[...]).astype(o_ref.dtype)


def _norm_matmul(x2d, gain, w, out_dtype, *, tm, tn):
    m, d = x2d.shape
    n = w.shape[1]
    return pl.pallas_call(
        _norm_matmul_body,
        grid=(m // tm, n // tn),
        in_specs=[
            pl.BlockSpec((tm, d), lambda i, j: (i, 0)),
            pl.BlockSpec((1, d), lambda i, j: (0, 0)),
            pl.BlockSpec((d, tn), lambda i, j: (0, j)),
        ],
        out_specs=pl.BlockSpec((tm, tn), lambda i, j: (i, j)),
        out_shape=jax.ShapeDtypeStruct((m, n), out_dtype),
        scratch_shapes=[pltpu.VMEM((tm, d), BF16)],
        compiler_params=_params(("parallel", "arbitrary")),
        name="norm_matmul",
    )(x2d, gain.reshape(1, d), w)


def _matmul_res_body(a_ref, w_ref, x_ref, o_ref):
    o_ref[...] = x_ref[...] + _dot(a_ref[...], w_ref[...])


def _matmul_res(a2d, w, x2d, *, tm):
    m, k = a2d.shape
    d = w.shape[1]
    return pl.pallas_call(
        _matmul_res_body,
        grid=(m // tm,),
        in_specs=[
            pl.BlockSpec((tm, k), lambda i: (i, 0)),
            pl.BlockSpec((k, d), lambda i: (0, 0)),
            pl.BlockSpec((tm, d), lambda i: (i, 0)),
        ],
        out_specs=pl.BlockSpec((tm, d), lambda i: (i, 0)),
        out_shape=jax.ShapeDtypeStruct((m, d), F32),
        compiler_params=_params(("parallel",)),
        name="matmul_res",
    )(a2d, w, x2d)


def _rows(start, dilation):
    if dilation == 1:
        return pl.ds(pl.multiple_of(start, N_BACK), N_BACK)
    return pl.ds(start, N_BACK, stride=dilation)


def _dilated_attn_body(slopes_ref, *refs, seq):
    n_groups = len(DILATION_PATTERNS)
    qkv_refs = refs[:3 * n_groups]
    o_ref, og_ref, lse_ref = refs[3 * n_groups:]
    slope = slopes_ref[pl.program_id(1)]
    scale = HEAD_DIM ** -0.5

    qi = lax.broadcasted_iota(jnp.int32, (N_BACK, N_BACK), 0)
    ki = lax.broadcasted_iota(jnp.int32, (N_BACK, N_BACK), 1)
    steps_cur = (qi - ki).astype(F32)
    steps_prev = steps_cur + float(N_BACK)

    for g, (window, dilation) in enumerate(DILATION_PATTERNS):
        q_ref, k_ref, v_ref = qkv_refs[3 * g:3 * g + 3]
        sub_len = seq // dilation
        n_blk = sub_len // N_BACK
        coef = -slope * float(dilation)
        bias_cur = jnp.where(ki <= qi, coef * steps_cur, MASK_VALUE)
        bias_prev = jnp.where(ki >= qi, coef * steps_prev, MASK_VALUE)

        def block(start, has_prev, q_ref=q_ref, k_ref=k_ref, v_ref=v_ref, g=g, dilation=dilation,
                  bias_cur=bias_cur, bias_prev=bias_prev):
            rows = _rows(start, dilation)
            q = q_ref[rows, :].astype(BF16)
            s_cur = _dot_nt(q, k_ref[rows, :].astype(BF16)) * scale + bias_cur
            m = jnp.max(s_cur, axis=-1, keepdims=True)
            if has_prev:
                prev_rows = _rows(start - N_BACK * dilation, dilation)
                s_prev = _dot_nt(q, k_ref[prev_rows, :].astype(BF16)) * scale + bias_prev
                m = jnp.maximum(m, jnp.max(s_prev, axis=-1, keepdims=True))
            p_cur = jnp.exp(s_cur - m)
            l = jnp.sum(p_cur, axis=-1, keepdims=True)
            o = _dot(p_cur.astype(BF16), v_ref[rows, :].astype(BF16))
            if has_prev:
                p_prev = jnp.exp(s_prev - m)
                l = l + jnp.sum(p_prev, axis=-1, keepdims=True)
                o = o + _dot(p_prev.astype(BF16), v_ref[prev_rows, :].astype(BF16))
            og_ref[g, rows, :] = o / l
            lse_ref[g, rows, :] = jnp.broadcast_to(m + jnp.log(l), (N_BACK, HEAD_DIM))

        def first_blocks(c, carry, block=block):
            block(c, False)
            return carry

        lax.fori_loop(0, dilation, first_blocks, 0)
        if n_blk > 1:
            def later_blocks(idx, carry, block=block, dilation=dilation, n_blk=n_blk):
                c = idx // (n_blk - 1)
                n = idx % (n_blk - 1) + 1
                block(c + n * N_BACK * dilation, True)
                return carry

            lax.fori_loop(0, dilation * (n_blk - 1), later_blocks, 0)

    chunk = 256

    def merge(i, carry):
        rows = pl.ds(pl.multiple_of(i * chunk, chunk), chunk)
        lses = [lse_ref[g, rows, :] for g in range(n_groups)]
        top = functools.reduce(jnp.maximum, lses)
        ws = [jnp.exp(l - top) for l in lses]
        num = sum(w * og_ref[g, rows, :] for g, w in enumerate(ws))
        o_ref[rows, :] = (num / sum(ws)).astype(o_ref.dtype)
        return carry

    lax.fori_loop(0, seq // chunk, merge, 0)


def _dilated_attn(qkv, slopes, *, n_heads):
    b, s, _ = qkv.shape
    n_groups = len(DILATION_PATTERNS)

    def col_spec(which, g):
        base = (which * n_groups + g) * n_heads
        return pl.BlockSpec((None, s, HEAD_DIM), lambda bi, h, base=base: (bi, 0, base + h))

    in_specs = [pl.BlockSpec(memory_space=pltpu.SMEM)]
    for g in range(n_groups):
        in_specs += [col_spec(0, g), col_spec(1, g), col_spec(2, g)]
    return pl.pallas_call(
        functools.partial(_dilated_attn_body, seq=s),
        grid=(b, n_heads),
        in_specs=in_specs,
        out_specs=pl.BlockSpec((None, s, HEAD_DIM), lambda bi, h: (bi, 0, h)),
        out_shape=jax.ShapeDtypeStruct((b, s, n_heads * HEAD_DIM), BF16),
        scratch_shapes=[pltpu.VMEM((n_groups, s, HEAD_DIM), F32),
                        pltpu.VMEM((n_groups, s, HEAD_DIM), F32)],
        compiler_params=_params(("parallel", "parallel")),
        name="dilated_attn",
    )(slopes, *([qkv] * (3 * n_groups)))


def _diff_attn_body(slopes_ref, lam_ref, q_ref, k_ref, v_ref, gain_ref, o_ref, *, tq, lambda_init):
    h = pl.program_id(1)
    qb = pl.program_id(2)
    slope = slopes_ref[h]
    scale = HEAD_DIM ** -0.5
    vdim = 2 * HEAD_DIM

    lam = lam_ref[...]
    lam_full = (jnp.exp(jnp.sum(lam[0:1] * lam[1:2], axis=-1, keepdims=True))
                - jnp.exp(jnp.sum(lam[2:3] * lam[3:4], axis=-1, keepdims=True))
                + lambda_init)

    q1 = q_ref[:, :HEAD_DIM]
    q2 = q_ref[:, HEAD_DIM:]
    qi = lax.broadcasted_iota(jnp.int32, (tq, tq), 0)
    ki = lax.broadcasted_iota(jnp.int32, (tq, tq), 1)
    rel_bias = -slope * (qi - ki).astype(F32)
    diag_bias = jnp.where(qi >= ki, rel_bias, MASK_VALUE)

    def attend(kb, carry, bias):
        rows = pl.ds(pl.multiple_of(kb * tq, tq), tq)
        v = v_ref[rows, :]
        new = []
        for q, cols, (m, l, acc) in ((q1, slice(0, HEAD_DIM), carry[0]),
                                     (q2, slice(HEAD_DIM, vdim), carry[1])):
            s = _dot_nt(q, k_ref[rows, cols]) * scale + bias
            m_new = jnp.maximum(m, jnp.max(s, axis=-1, keepdims=True))
            alpha = jnp.exp(m - m_new)
            p = jnp.exp(s - m_new)
            l = alpha * l + jnp.sum(p, axis=-1, keepdims=True)
            acc = alpha * acc + _dot(p.astype(BF16), v)
            new.append((m_new, l, acc))
        return tuple(new)

    def off_diagonal(kb, carry):
        offset = -slope * ((qb - kb) * tq).astype(F32)
        return attend(kb, carry, rel_bias + offset)

    init = (jnp.full((tq, 1), MASK_VALUE, F32), jnp.zeros((tq, 1), F32), jnp.zeros((tq, vdim), F32))
    carry = lax.fori_loop(0, qb, off_diagonal, (init, init))
    (_, l1, acc1), (_, l2, acc2) = attend(qb, carry, diag_bias)

    o = acc1 / l1 - lam_full * (acc2 / l2)
    o = _rms_normed(o, gain_ref[...], SUBLN_EPS) * (1.0 - lambda_init)
    o_ref[...] = o.astype(o_ref.dtype)


def _diff_attn(q, kv, slopes, lam, subln_gain, *, n_heads, lambda_init, tq):
    b, s, _ = q.shape
    vdim = 2 * HEAD_DIM
    return pl.pallas_call(
        functools.partial(_diff_attn_body, tq=tq, lambda_init=lambda_init),
        grid=(b, n_heads, s // tq),
        in_specs=[
            pl.BlockSpec(memory_space=pltpu.SMEM),
            pl.BlockSpec((4, HEAD_DIM), lambda bi, h, i: (0, 0)),
            pl.BlockSpec((None, tq, vdim), lambda bi, h, i: (bi, i, h)),
            pl.BlockSpec((None, s, vdim), lambda bi, h, i: (bi, 0, h)),
            pl.BlockSpec((None, s, vdim), lambda bi, h, i: (bi, 0, n_heads + h)),
            pl.BlockSpec((1, vdim), lambda bi, h, i: (0, 0)),
        ],
        out_specs=pl.BlockSpec((None, tq, vdim), lambda bi, h, i: (bi, i, h)),
        out_shape=jax.ShapeDtypeStruct((b, s, n_heads * vdim), BF16),
        compiler_params=_params(("parallel", "parallel", "arbitrary")),
        name="diff_attn",
    )(slopes, lam, q, kv, kv, subln_gain.reshape(1, vdim))


def _alibi_slopes(n_heads):
    return 2.0 ** (-8.0 * jnp.arange(1, n_heads + 1, dtype=F32) / n_heads)


def _pad_to(a, axis, size):
    pad = [(0, 0)] * a.ndim
    pad[axis] = (0, size - a.shape[axis])
    return jnp.pad(a, pad)


def _tile(total, preferred):
    return preferred if total % preferred == 0 else total


def kernel(x, norm_gains, ffn_w_in, ffn_w_out, a_w_qkv, a_w_o, kv_norm_gain, b_w_kv, b_w_q, b_lambda,
           b_subln_gain, b_w_o, final_norm_gain):
    b, s, d = x.shape
    m = b * s
    depth = norm_gains.shape[0]
    n_a_layers = a_w_qkv.shape[0]
    d_ff = ffn_w_out.shape[2]
    heads_a = a_w_o.shape[1] // HEAD_DIM
    heads_b = b_w_o.shape[1] // (2 * HEAD_DIM)

    tm = _tile(m, 512)
    tf = 512
    d_ff_pad = -(-d_ff // tf) * tf

    slopes_a = _alibi_slopes(heads_a)
    slopes_b = _alibi_slopes(heads_b)

    def ffn(xs, layer, idx, final_gain=None):
        w_in = ffn_w_in[layer, idx]
        w_gate = _pad_to(w_in[:, :d_ff].astype(BF16), 1, d_ff_pad)
        w_up = _pad_to(w_in[:, d_ff:].astype(BF16), 1, d_ff_pad)
        w_out = _pad_to(ffn_w_out[layer, idx].astype(BF16), 0, d_ff_pad)
        return _ffn(xs, norm_gains[layer, 0 if idx == 0 else 2], w_gate, w_up, w_out, final_gain,
                    tm=tm, tf=tf)

    xs = x.reshape(m, d)
    kv = None
    for layer in range(depth):
        xs = ffn(xs, layer, 0)
        if layer < n_a_layers:
            qkv = _norm_matmul(xs, norm_gains[layer, 1], a_w_qkv[layer].astype(BF16), F32,
                               tm=tm, tn=_tile(a_w_qkv.shape[2], 1024))
            merged = _dilated_attn(qkv.reshape(b, s, -1), slopes_a, n_heads=heads_a)
            xs = _matmul_res(merged.reshape(m, -1), a_w_o[layer].astype(BF16), xs, tm=tm)
        else:
            j = layer - n_a_layers
            q = _norm_matmul(xs, norm_gains[layer, 1], b_w_q[j].astype(BF16), BF16,
                             tm=tm, tn=_tile(b_w_q.shape[2], 1024))
            lambda_init = 0.8 - 0.6 * math.exp(-0.3 * layer)
            o = _diff_attn(q.reshape(b, s, -1), kv, slopes_b, b_lambda[j], b_subln_gain[j],
                           n_heads=heads_b, lambda_init=lambda_init, tq=_tile(s, 256))
            xs = _matmul_res(o.reshape(m, -1), b_w_o[j].astype(BF16), xs, tm=tm)
        last = layer == depth - 1
        xs = ffn(xs, layer, 1, final_norm_gain if last else None)
        if layer == n_a_layers - 1:
            kv = _norm_matmul(xs, kv_norm_gain, b_w_kv.astype(BF16), BF16,
                              tm=tm, tn=_tile(b_w_kv.shape[1], 1024)).reshape(b, s, -1)
    return xs.reshape(b, s, d)
```

```python
import functools
import math

import jax
import jax.numpy as jnp
from jax import lax
from jax.experimental import pallas as pl
from jax.experimental.pallas import tpu as pltpu

F32 = jnp.float32
BF16 = jnp.bfloat16

HEAD_DIM = 128
DILATION_PATTERNS = ((128, 1), (512, 4), (2048, 16))
N_BACK = 128
MACARON_WEIGHT = 0.5
RMS_EPS = 1e-6
SUBLN_EPS = 1e-5
LOG2_E = math.log2(math.e)
MASK_VALUE = -1e30
V7X_VMEM_LIMIT_BYTES = 56 * 1024 * 1024


def _params(semantics):
    return pltpu.CompilerParams(dimension_semantics=semantics,
                                vmem_limit_bytes=V7X_VMEM_LIMIT_BYTES)


def _rms_normed(x, gain, eps):
    y = x * lax.rsqrt(jnp.mean(x * x, axis=-1, keepdims=True) + eps)
    return y * gain


def _dot(a, b):
    return jnp.dot(a, b, preferred_element_type=F32)


def _dot_nt(a, b):
    return lax.dot_general(a, b, (((1,), (1,)), ((), ())), preferred_element_type=F32)


def _ffn_body(x_ref, g_ref, wg_ref, wu_ref, wo_ref, fg_ref, o_ref, xn_ref, *, final_norm):
    f = pl.program_id(1)

    @pl.when(f == 0)
    def _():
        x = x_ref[...]
        xn_ref[...] = _rms_normed(x, g_ref[...], RMS_EPS).astype(BF16)
        o_ref[...] = x

    xn = xn_ref[...]
    gate = _dot(xn, wg_ref[...])
    up = _dot(xn, wu_ref[...])
    h = (gate * jax.nn.sigmoid(gate) * up * MACARON_WEIGHT).astype(BF16)
    o_ref[...] += _dot(h, wo_ref[...])

    if final_norm:
        @pl.when(f == pl.num_programs(1) - 1)
        def _():
            o_ref[...] = _rms_normed(o_ref[...], fg_ref[...], RMS_EPS)


def _ffn(x2d, gain, w_gate, w_up, w_out, final_gain, *, layer, idx, tm, tf):
    m, d = x2d.shape
    fp = w_gate.shape[-1]
    final_norm = final_gain is not None
    fg = final_gain if final_norm else gain
    return pl.pallas_call(
        functools.partial(_ffn_body, final_norm=final_norm),
        grid=(m // tm, fp // tf),
        in_specs=[
            pl.BlockSpec((tm, d), lambda i, f: (i, 0)),
            pl.BlockSpec((1, d), lambda i, f: (0, 0)),
            pl.BlockSpec((None, None, d, tf), lambda i, f: (layer, idx, 0, f)),
            pl.BlockSpec((None, None, d, tf), lambda i, f: (layer, idx, 0, f)),
            pl.BlockSpec((None, None, tf, d), lambda i, f: (layer, idx, f, 0)),
            pl.BlockSpec((1, d), lambda i, f: (0, 0)),
        ],
        out_specs=pl.BlockSpec((tm, d), lambda i, f: (i, 0)),
        out_shape=jax.ShapeDtypeStruct((m, d), F32),
        scratch_shapes=[pltpu.VMEM((tm, d), BF16)],
        compiler_params=_params(("parallel", "arbitrary")),
        name="ffn",
    )(x2d, gain.reshape(1, d), w_gate, w_up, w_out, fg.reshape(1, d))


def _norm_matmul_body(x_ref, g_ref, w_ref, o_ref, xn_ref, *, out_scale):
    @pl.when(pl.program_id(1) == 0)
    def _():
        xn_ref[...] = _rms_normed(x_ref[...], g_ref[...], RMS_EPS).astype(BF16)

    acc = _dot(xn_ref[...], w_ref[...])
    if out_scale is not None:
        acc = acc * out_scale
    o_ref[...] = acc.astype(o_ref.dtype)


def _norm_matmul(x2d, gain, w, out_dtype, *, tm, tn, out_scale=None):
    m, d = x2d.shape
    n = w.shape[1]
    return pl.pallas_call(
        functools.partial(_norm_matmul_body, out_scale=out_scale),
        grid=(m // tm, n // tn),
        in_specs=[
            pl.BlockSpec((tm, d), lambda i, j: (i, 0)),
            pl.BlockSpec((1, d), lambda i, j: (0, 0)),
            pl.BlockSpec((d, tn), lambda i, j: (0, j)),
        ],
        out_specs=pl.BlockSpec((tm, tn), lambda i, j: (i, j)),
        out_shape=jax.ShapeDtypeStruct((m, n), out_dtype),
        scratch_shapes=[pltpu.VMEM((tm, d), BF16)],
        compiler_params=_params(("parallel", "arbitrary")),
        name="norm_matmul",
    )(x2d, gain.reshape(1, d), w)


def _matmul_res_body(a_ref, w_ref, x_ref, o_ref):
    o_ref[...] = x_ref[...] + _dot(a_ref[...], w_ref[...])


def _matmul_res(a2d, w, x2d, *, tm):
    m, k = a2d.shape
    d = w.shape[1]
    return pl.pallas_call(
        _matmul_res_body,
        grid=(m // tm,),
        in_specs=[
            pl.BlockSpec((tm, k), lambda i: (i, 0)),
            pl.BlockSpec((k, d), lambda i: (0, 0)),
            pl.BlockSpec((tm, d), lambda i: (i, 0)),
        ],
        out_specs=pl.BlockSpec((tm, d), lambda i: (i, 0)),
        out_shape=jax.ShapeDtypeStruct((m, d), F32),
        compiler_params=_params(("parallel",)),
        name="matmul_res",
    )(a2d, w, x2d)


def _bmm(a, b):
    return lax.dot_general(a, b, (((2,), (1,)), ((0,), (0,))), preferred_element_type=F32)


def _bmm_nt(a, b):
    return lax.dot_general(a, b, (((2,), (2,)), ((0,), (0,))), preferred_element_type=F32)


def _residue_rows(c, sub_len, dilation):
    return pl.ds(c, sub_len) if dilation == 1 else pl.ds(c, sub_len, stride=dilation)


def _dilated_attn_body(slopes_ref, *refs, seq):
    n_groups = len(DILATION_PATTERNS)
    qkv_refs = refs[:3 * n_groups]
    o_ref, og_ref, lse_ref = refs[3 * n_groups:]
    slope = slopes_ref[pl.program_id(1)]
    q_scale = HEAD_DIM ** -0.5 * LOG2_E

    qi = lax.broadcasted_iota(jnp.int32, (N_BACK, N_BACK), 0)
    ki = lax.broadcasted_iota(jnp.int32, (N_BACK, N_BACK), 1)
    steps_cur = (qi - ki).astype(F32)
    steps_prev = steps_cur + float(N_BACK)

    outs, lses = [], []
    for g, (window, dilation) in enumerate(DILATION_PATTERNS):
        q_ref, k_ref, v_ref = qkv_refs[3 * g:3 * g + 3]
        sub_len = seq // dilation
        n_blk = sub_len // N_BACK

        def blocks(ref):
            parts = [ref[_residue_rows(c, sub_len, dilation), :].reshape(n_blk, N_BACK, HEAD_DIM)
                     for c in range(dilation)]
            return parts[0] if dilation == 1 else jnp.concatenate(parts, axis=0)

        def later(a):
            return jnp.concatenate([a[c * n_blk + 1:(c + 1) * n_blk] for c in range(dilation)], axis=0)

        def earlier(a):
            return jnp.concatenate([a[c * n_blk:(c + 1) * n_blk - 1] for c in range(dilation)], axis=0)

        def pad_first(a, fill):
            first = jnp.full((1,) + a.shape[1:], fill, a.dtype)
            parts = []
            for c in range(dilation):
                parts += [first, a[c * (n_blk - 1):(c + 1) * (n_blk - 1)]]
            return jnp.concatenate(parts, axis=0)

        q = (blocks(q_ref) * q_scale).astype(BF16)
        k = blocks(k_ref).astype(BF16)
        v = blocks(v_ref).astype(BF16)
        v_ones = jnp.concatenate([v, jnp.ones_like(v)], axis=-1)

        coef = -slope * (float(dilation) * LOG2_E)
        s_cur = _bmm_nt(q, k) + jnp.where(ki <= qi, coef * steps_cur, MASK_VALUE)
        m = jnp.max(s_cur, axis=-1, keepdims=True)
        if n_blk > 1:
            s_prev = _bmm_nt(later(q), earlier(k)) + jnp.where(ki >= qi, coef * steps_prev, MASK_VALUE)
            m = jnp.maximum(m, pad_first(jnp.max(s_prev, axis=-1, keepdims=True), MASK_VALUE))
            p_prev = jnp.exp2(s_prev - later(m)).astype(BF16)
        ol = _bmm(jnp.exp2(s_cur - m).astype(BF16), v_ones)
        if n_blk > 1:
            ol = ol + pad_first(_bmm(p_prev, earlier(v_ones)), 0.0)
        denom = ol[..., HEAD_DIM:]
        o = ol[..., :HEAD_DIM] / denom
        lse = m + jnp.log2(denom)
        if dilation == 1:
            outs.append(o.reshape(seq, HEAD_DIM))
            lses.append(lse.reshape(seq, HEAD_DIM))
        else:
            for c in range(dilation):
                rows = _residue_rows(c, sub_len, dilation)
                og_ref[g - 1, rows, :] = o[c * n_blk:(c + 1) * n_blk].reshape(sub_len, HEAD_DIM)
                lse_ref[g - 1, rows, :] = lse[c * n_blk:(c + 1) * n_blk].reshape(sub_len, HEAD_DIM)
            outs.append(og_ref[g - 1])
            lses.append(lse_ref[g - 1])

    top = functools.reduce(jnp.maximum, lses)
    ws = [jnp.exp2(l - top) for l in lses]
    num = sum(w * o for w, o in zip(ws, outs))
    o_ref[...] = (num / sum(ws)).astype(o_ref.dtype)


def _dilated_attn(qkv, slopes, *, n_heads):
    b, s, _ = qkv.shape
    n_groups = len(DILATION_PATTERNS)

    def col_spec(which, g):
        base = (which * n_groups + g) * n_heads
        return pl.BlockSpec((None, s, HEAD_DIM), lambda bi, h, base=base: (bi, 0, base + h))

    in_specs = [pl.BlockSpec(memory_space=pltpu.SMEM)]
    for g in range(n_groups):
        in_specs += [col_spec(0, g), col_spec(1, g), col_spec(2, g)]
    return pl.pallas_call(
        functools.partial(_dilated_attn_body, seq=s),
        grid=(b, n_heads),
        in_specs=in_specs,
        out_specs=pl.BlockSpec((None, s, HEAD_DIM), lambda bi, h: (bi, 0, h)),
        out_shape=jax.ShapeDtypeStruct((b, s, n_heads * HEAD_DIM), BF16),
        scratch_shapes=[pltpu.VMEM((n_groups - 1, s, HEAD_DIM), F32),
                        pltpu.VMEM((n_groups - 1, s, HEAD_DIM), F32)],
        compiler_params=_params(("parallel", "parallel")),
        name="dilated_attn",
    )(slopes, *([qkv] * (3 * n_groups)))


def _diff_attn_body(slopes_ref, lam_ref, q_ref, k_ref, v_ref, gain_ref, o_ref, *, seq, tq, lambda_init):
    slope = slopes_ref[pl.program_id(1)] * LOG2_E
    vdim = 2 * HEAD_DIM

    lam = lam_ref[...]
    lam_full = (jnp.exp(jnp.sum(lam[0:1] * lam[1:2], axis=-1, keepdims=True))
                - jnp.exp(jnp.sum(lam[2:3] * lam[3:4], axis=-1, keepdims=True))
                + lambda_init)

    key_bias = slope * lax.broadcasted_iota(jnp.int32, (1, seq), 1).astype(F32)
    qi = lax.broadcasted_iota(jnp.int32, (tq, tq), 0)
    ki = lax.broadcasted_iota(jnp.int32, (tq, tq), 1)
    causal = jnp.where(qi >= ki, 0.0, MASK_VALUE)

    for i in range(seq // tq):
        lo, hi = i * tq, (i + 1) * tq
        probs, denoms = [], []
        for cols in (slice(0, HEAD_DIM), slice(HEAD_DIM, vdim)):
            q = q_ref[lo:hi, cols]
            s_diag = _dot_nt(q, k_ref[lo:hi, cols]) + (key_bias[:, lo:hi] + causal)
            m = jnp.max(s_diag, axis=-1, keepdims=True)
            if i > 0:
                s_off = _dot_nt(q, k_ref[0:lo, cols]) + key_bias[:, 0:lo]
                m = jnp.maximum(m, jnp.max(s_off, axis=-1, keepdims=True))
            p_diag = jnp.exp2(s_diag - m)
            l = jnp.sum(p_diag, axis=-1, keepdims=True)
            p = p_diag.astype(BF16)
            if i > 0:
                p_off = jnp.exp2(s_off - m)
                l = l + jnp.sum(p_off, axis=-1, keepdims=True)
                p = jnp.concatenate([p_off.astype(BF16), p], axis=1)
            probs.append(p)
            denoms.append(l)
        pv = _dot(jnp.concatenate(probs, axis=0), v_ref[0:hi, :])
        o = pv[:tq] / denoms[0] - lam_full * (pv[tq:] / denoms[1])
        o = _rms_normed(o, gain_ref[...], SUBLN_EPS) * (1.0 - lambda_init)
        o_ref[lo:hi, :] = o.astype(o_ref.dtype)


def _diff_attn(q, kv, slopes, lam, subln_gain, *, n_heads, lambda_init, tq):
    b, s, _ = q.shape
    vdim = 2 * HEAD_DIM
    return pl.pallas_call(
        functools.partial(_diff_attn_body, seq=s, tq=tq, lambda_init=lambda_init),
        grid=(b, n_heads),
        in_specs=[
            pl.BlockSpec(memory_space=pltpu.SMEM),
            pl.BlockSpec((4, HEAD_DIM), lambda bi, h: (0, 0)),
            pl.BlockSpec((None, s, vdim), lambda bi, h: (bi, 0, h)),
            pl.BlockSpec((None, s, vdim), lambda bi, h: (bi, 0, h)),
            pl.BlockSpec((None, s, vdim), lambda bi, h: (bi, 0, n_heads + h)),
            pl.BlockSpec((1, vdim), lambda bi, h: (0, 0)),
        ],
        out_specs=pl.BlockSpec((None, s, vdim), lambda bi, h: (bi, 0, h)),
        out_shape=jax.ShapeDtypeStruct((b, s, n_heads * vdim), BF16),
        compiler_params=_params(("parallel", "parallel")),
        name="diff_attn",
    )(slopes, lam, q, kv, kv, subln_gain.reshape(1, vdim))


def _alibi_slopes(n_heads):
    return 2.0 ** (-8.0 * jnp.arange(1, n_heads + 1, dtype=F32) / n_heads)


def _pad_to(a, axis, size):
    pad = [(0, 0)] * a.ndim
    pad[axis] = (0, size - a.shape[axis])
    return jnp.pad(a, pad)


def _tile(total, preferred):
    return preferred if total % preferred == 0 else total


def kernel(x, norm_gains, ffn_w_in, ffn_w_out, a_w_qkv, a_w_o, kv_norm_gain, b_w_kv, b_w_q, b_lambda,
           b_subln_gain, b_w_o, final_norm_gain):
    b, s, d = x.shape
    m = b * s
    depth = norm_gains.shape[0]
    n_a_layers = a_w_qkv.shape[0]
    d_ff = ffn_w_out.shape[2]
    heads_a = a_w_o.shape[1] // HEAD_DIM
    heads_b = b_w_o.shape[1] // (2 * HEAD_DIM)

    tm = _tile(m, 512)
    tm_proj = _tile(m, 1024)
    tf = 512
    d_ff_pad = -(-d_ff // tf) * tf

    slopes_a = _alibi_slopes(heads_a)
    slopes_b = _alibi_slopes(heads_b)

    w_gate = _pad_to(ffn_w_in[..., :d_ff].astype(BF16), 3, d_ff_pad)
    w_up = _pad_to(ffn_w_in[..., d_ff:].astype(BF16), 3, d_ff_pad)
    w_out = _pad_to(ffn_w_out.astype(BF16), 2, d_ff_pad)

    def ffn(xs, layer, idx, final_gain=None):
        return _ffn(xs, norm_gains[layer, 0 if idx == 0 else 2], w_gate, w_up, w_out, final_gain,
                    layer=layer, idx=idx, tm=tm, tf=tf)

    xs = x.reshape(m, d)
    kv = None
    for layer in range(depth):
        xs = ffn(xs, layer, 0)
        if layer < n_a_layers:
            qkv = _norm_matmul(xs, norm_gains[layer, 1], a_w_qkv[layer].astype(BF16), F32,
                               tm=tm_proj, tn=_tile(a_w_qkv.shape[2], 1024))
            merged = _dilated_attn(qkv.reshape(b, s, -1), slopes_a, n_heads=heads_a)
            xs = _matmul_res(merged.reshape(m, -1), a_w_o[layer].astype(BF16), xs, tm=tm)
        else:
            j = layer - n_a_layers
            q = _norm_matmul(xs, norm_gains[layer, 1], b_w_q[j].astype(BF16), BF16,
                             tm=tm_proj, tn=_tile(b_w_q.shape[2], 1024),
                             out_scale=HEAD_DIM ** -0.5 * LOG2_E)
            lambda_init = 0.8 - 0.6 * math.exp(-0.3 * layer)
            o = _diff_attn(q.reshape(b, s, -1), kv, slopes_b, b_lambda[j], b_subln_gain[j],
                           n_heads=heads_b, lambda_init=lambda_init, tq=_tile(s, 256))
            xs = _matmul_res(o.reshape(m, -1), b_w_o[j].astype(BF16), xs, tm=tm)
        last = layer == depth - 1
        xs = ffn(xs, layer, 1, final_norm_gain if last else None)
        if layer == n_a_layers - 1:
            kv = _norm_matmul(xs, kv_norm_gain, b_w_kv.astype(BF16), BF16,
                              tm=tm_proj, tn=_tile(b_w_kv.shape[1], 1024)).reshape(b, s, -1)
    return xs.reshape(b, s, d)
```

```python
import functools
import math

import jax
import jax.numpy as jnp
from jax import lax
from jax.experimental import pallas as pl
from jax.experimental.pallas import tpu as pltpu

F32 = jnp.float32
BF16 = jnp.bfloat16

HEAD_DIM = 128
DILATION_PATTERNS = ((128, 1), (512, 4), (2048, 16))
N_BACK = 128
MACARON_WEIGHT = 0.5
RMS_EPS = 1e-6
SUBLN_EPS = 1e-5
LOG2_E = math.log2(math.e)
MASK_VALUE = -1e30
V7X_VMEM_LIMIT_BYTES = 56 * 1024 * 1024


def _params(semantics):
    return pltpu.CompilerParams(dimension_semantics=semantics,
                                vmem_limit_bytes=V7X_VMEM_LIMIT_BYTES)


def _rms_normed(x, gain, eps):
    y = x * lax.rsqrt(jnp.mean(x * x, axis=-1, keepdims=True) + eps)
    return y * gain


def _dot(a, b):
    return jnp.dot(a, b, preferred_element_type=F32)


def _dot_nt(a, b):
    return lax.dot_general(a, b, (((1,), (1,)), ((), ())), preferred_element_type=F32)


def _ffn_body(x_ref, g_ref, wg_ref, wu_ref, wo_ref, fg_ref, o_ref, xn_ref, *, final_norm):
    f = pl.program_id(1)

    @pl.when(f == 0)
    def _():
        x = x_ref[...]
        xn_ref[...] = _rms_normed(x, g_ref[...], RMS_EPS).astype(BF16)
        o_ref[...] = x

    xn = xn_ref[...]
    gate = _dot(xn, wg_ref[...])
    up = _dot(xn, wu_ref[...])
    h = (gate * jax.nn.sigmoid(gate) * up * MACARON_WEIGHT).astype(BF16)
    o_ref[...] += _dot(h, wo_ref[...])

    if final_norm:
        @pl.when(f == pl.num_programs(1) - 1)
        def _():
            o_ref[...] = _rms_normed(o_ref[...], fg_ref[...], RMS_EPS)


def _ffn(x2d, gain, w_in, w_out, final_gain, *, layer, idx, tm, tf):
    m, d = x2d.shape
    fp = w_out.shape[2]
    n_f = fp // tf
    final_norm = final_gain is not None
    fg = final_gain if final_norm else gain
    return pl.pallas_call(
        functools.partial(_ffn_body, final_norm=final_norm),
        grid=(m // tm, n_f),
        in_specs=[
            pl.BlockSpec((tm, d), lambda i, f: (i, 0)),
            pl.BlockSpec((1, d), lambda i, f: (0, 0)),
            pl.BlockSpec((None, None, d, tf), lambda i, f: (layer, idx, 0, f)),
            pl.BlockSpec((None, None, d, tf), lambda i, f: (layer, idx, 0, n_f + f)),
            pl.BlockSpec((None, None, tf, d), lambda i, f: (layer, idx, f, 0)),
            pl.BlockSpec((1, d), lambda i, f: (0, 0)),
        ],
        out_specs=pl.BlockSpec((tm, d), lambda i, f: (i, 0)),
        out_shape=jax.ShapeDtypeStruct((m, d), F32),
        scratch_shapes=[pltpu.VMEM((tm, d), BF16)],
        compiler_params=_params(("parallel", "arbitrary")),
        name="ffn",
    )(x2d, gain.reshape(1, d), w_in, w_in, w_out, fg.reshape(1, d))


def _norm_matmul_body(x_ref, g_ref, w_ref, o_ref, xn_ref, *, out_scale):
    @pl.when(pl.program_id(1) == 0)
    def _():
        xn_ref[...] = _rms_normed(x_ref[...], g_ref[...], RMS_EPS).astype(BF16)

    acc = _dot(xn_ref[...], w_ref[...])
    if out_scale is not None:
        acc = acc * out_scale
    o_ref[...] = acc.astype(o_ref.dtype)


def _norm_matmul(x2d, gain, w, out_dtype, *, tm, tn, out_scale=None):
    m, d = x2d.shape
    n = w.shape[1]
    return pl.pallas_call(
        functools.partial(_norm_matmul_body, out_scale=out_scale),
        grid=(m // tm, n // tn),
        in_specs=[
            pl.BlockSpec((tm, d), lambda i, j: (i, 0)),
            pl.BlockSpec((1, d), lambda i, j: (0, 0)),
            pl.BlockSpec((d, tn), lambda i, j: (0, j)),
        ],
        out_specs=pl.BlockSpec((tm, tn), lambda i, j: (i, j)),
        out_shape=jax.ShapeDtypeStruct((m, n), out_dtype),
        scratch_shapes=[pltpu.VMEM((tm, d), BF16)],
        compiler_params=_params(("parallel", "arbitrary")),
        name="norm_matmul",
    )(x2d, gain.reshape(1, d), w)


def _matmul_res_body(a_ref, w_ref, x_ref, o_ref):
    o_ref[...] = x_ref[...] + _dot(a_ref[...], w_ref[...])


def _matmul_res(a2d, w, x2d, *, tm):
    m, k = a2d.shape
    d = w.shape[1]
    return pl.pallas_call(
        _matmul_res_body,
        grid=(m // tm,),
        in_specs=[
            pl.BlockSpec((tm, k), lambda i: (i, 0)),
            pl.BlockSpec((k, d), lambda i: (0, 0)),
            pl.BlockSpec((tm, d), lambda i: (i, 0)),
        ],
        out_specs=pl.BlockSpec((tm, d), lambda i: (i, 0)),
        out_shape=jax.ShapeDtypeStruct((m, d), F32),
        compiler_params=_params(("parallel",)),
        name="matmul_res",
    )(a2d, w, x2d)


def _bmm(a, b):
    return lax.dot_general(a, b, (((2,), (1,)), ((0,), (0,))), preferred_element_type=F32)


def _bmm_nt(a, b):
    return lax.dot_general(a, b, (((2,), (2,)), ((0,), (0,))), preferred_element_type=F32)


def _residue_rows(c, sub_len, dilation):
    return pl.ds(c, sub_len) if dilation == 1 else pl.ds(c, sub_len, stride=dilation)


def _dilated_attn_body(slopes_ref, *refs, seq):
    n_groups = len(DILATION_PATTERNS)
    qkv_refs = refs[:3 * n_groups]
    o_ref, og_ref, lse_ref = refs[3 * n_groups:]
    slope = slopes_ref[pl.program_id(1)]
    q_scale = HEAD_DIM ** -0.5 * LOG2_E

    qi = lax.broadcasted_iota(jnp.int32, (N_BACK, N_BACK), 0)
    ki = lax.broadcasted_iota(jnp.int32, (N_BACK, N_BACK), 1)
    steps_cur = (qi - ki).astype(F32)
    steps_prev = steps_cur + float(N_BACK)

    outs, lses = [], []
    for g, (window, dilation) in enumerate(DILATION_PATTERNS):
        q_ref, k_ref, v_ref = qkv_refs[3 * g:3 * g + 3]
        sub_len = seq // dilation
        n_blk = sub_len // N_BACK

        def blocks(ref):
            parts = [ref[_residue_rows(c, sub_len, dilation), :].reshape(n_blk, N_BACK, HEAD_DIM)
                     for c in range(dilation)]
            return parts[0] if dilation == 1 else jnp.concatenate(parts, axis=0)

        def later(a):
            return jnp.concatenate([a[c * n_blk + 1:(c + 1) * n_blk] for c in range(dilation)], axis=0)

        def earlier(a):
            return jnp.concatenate([a[c * n_blk:(c + 1) * n_blk - 1] for c in range(dilation)], axis=0)

        def pad_first(a, fill):
            first = jnp.full((1,) + a.shape[1:], fill, a.dtype)
            parts = []
            for c in range(dilation):
                parts += [first, a[c * (n_blk - 1):(c + 1) * (n_blk - 1)]]
            return jnp.concatenate(parts, axis=0)

        q = (blocks(q_ref) * q_scale).astype(BF16)
        k = blocks(k_ref).astype(BF16)
        v = blocks(v_ref).astype(BF16)
        v_ones = jnp.concatenate([v, jnp.ones_like(v)], axis=-1)

        coef = -slope * (float(dilation) * LOG2_E)
        s_cur = _bmm_nt(q, k) + jnp.where(ki <= qi, coef * steps_cur, MASK_VALUE)
        m = jnp.max(s_cur, axis=-1, keepdims=True)
        if n_blk > 1:
            s_prev = _bmm_nt(later(q), earlier(k)) + jnp.where(ki >= qi, coef * steps_prev, MASK_VALUE)
            m = jnp.maximum(m, pad_first(jnp.max(s_prev, axis=-1, keepdims=True), MASK_VALUE))
            p_prev = jnp.exp2(s_prev - later(m)).astype(BF16)
        ol = _bmm(jnp.exp2(s_cur - m).astype(BF16), v_ones)
        if n_blk > 1:
            ol = ol + pad_first(_bmm(p_prev, earlier(v_ones)), 0.0)
        denom = ol[..., HEAD_DIM:]
        o = ol[..., :HEAD_DIM] / denom
        lse = m + jnp.log2(denom)
        if dilation == 1:
            outs.append(o.reshape(seq, HEAD_DIM))
            lses.append(lse.reshape(seq, HEAD_DIM))
        else:
            for c in range(dilation):
                rows = _residue_rows(c, sub_len, dilation)
                og_ref[g - 1, rows, :] = o[c * n_blk:(c + 1) * n_blk].reshape(sub_len, HEAD_DIM)
                lse_ref[g - 1, rows, :] = lse[c * n_blk:(c + 1) * n_blk].reshape(sub_len, HEAD_DIM)
            outs.append(og_ref[g - 1])
            lses.append(lse_ref[g - 1])

    top = functools.reduce(jnp.maximum, lses)
    ws = [jnp.exp2(l - top) for l in lses]
    num = sum(w * o for w, o in zip(ws, outs))
    o_ref[...] = (num / sum(ws)).astype(o_ref.dtype)


def _dilated_attn(qkv, slopes, *, n_heads):
    b, s, _ = qkv.shape
    n_groups = len(DILATION_PATTERNS)

    def col_spec(which, g):
        base = (which * n_groups + g) * n_heads
        return pl.BlockSpec((None, s, HEAD_DIM), lambda bi, h, base=base: (bi, 0, base + h))

    in_specs = [pl.BlockSpec(memory_space=pltpu.SMEM)]
    for g in range(n_groups):
        in_specs += [col_spec(0, g), col_spec(1, g), col_spec(2, g)]
    return pl.pallas_call(
        functools.partial(_dilated_attn_body, seq=s),
        grid=(b, n_heads),
        in_specs=in_specs,
        out_specs=pl.BlockSpec((None, s, HEAD_DIM), lambda bi, h: (bi, 0, h)),
        out_shape=jax.ShapeDtypeStruct((b, s, n_heads * HEAD_DIM), BF16),
        scratch_shapes=[pltpu.VMEM((n_groups - 1, s, HEAD_DIM), F32),
                        pltpu.VMEM((n_groups - 1, s, HEAD_DIM), F32)],
        compiler_params=_params(("parallel", "parallel")),
        name="dilated_attn",
    )(slopes, *([qkv] * (3 * n_groups)))


def _diff_attn_body(slopes_ref, lam_ref, q_ref, k_ref, v_ref, gain_ref, o_ref, *, seq, tq, lambda_init):
    slope = slopes_ref[pl.program_id(1)] * LOG2_E
    vdim = 2 * HEAD_DIM

    lam = lam_ref[...]
    lam_full = (jnp.exp(jnp.sum(lam[0:1] * lam[1:2], axis=-1, keepdims=True))
                - jnp.exp(jnp.sum(lam[2:3] * lam[3:4], axis=-1, keepdims=True))
                + lambda_init)

    key_bias = slope * lax.broadcasted_iota(jnp.int32, (1, seq), 1).astype(F32)
    qi = lax.broadcasted_iota(jnp.int32, (tq, tq), 0)
    ki = lax.broadcasted_iota(jnp.int32, (tq, tq), 1)
    causal = jnp.where(qi >= ki, 0.0, MASK_VALUE)

    for i in range(seq // tq):
        lo, hi = i * tq, (i + 1) * tq
        probs, denoms = [], []
        for cols in (slice(0, HEAD_DIM), slice(HEAD_DIM, vdim)):
            q = q_ref[lo:hi, cols]
            s_diag = _dot_nt(q, k_ref[lo:hi, cols]) + (key_bias[:, lo:hi] + causal)
            m = jnp.max(s_diag, axis=-1, keepdims=True)
            if i > 0:
                s_off = _dot_nt(q, k_ref[0:lo, cols]) + key_bias[:, 0:lo]
                m = jnp.maximum(m, jnp.max(s_off, axis=-1, keepdims=True))
            p_diag = jnp.exp2(s_diag - m)
            l = jnp.sum(p_diag, axis=-1, keepdims=True)
            p = p_diag.astype(BF16)
            if i > 0:
                p_off = jnp.exp2(s_off - m)
                l = l + jnp.sum(p_off, axis=-1, keepdims=True)
                p = jnp.concatenate([p_off.astype(BF16), p], axis=1)
            probs.append(p)
            denoms.append(l)
        pv = _dot(jnp.concatenate(probs, axis=0), v_ref[0:hi, :])
        o = pv[:tq] / denoms[0] - lam_full * (pv[tq:] / denoms[1])
        o = _rms_normed(o, gain_ref[...], SUBLN_EPS) * (1.0 - lambda_init)
        o_ref[lo:hi, :] = o.astype(o_ref.dtype)


def _diff_attn(q, kv, slopes, lam, subln_gain, *, n_heads, lambda_init, tq):
    b, s, _ = q.shape
    vdim = 2 * HEAD_DIM
    return pl.pallas_call(
        functools.partial(_diff_attn_body, seq=s, tq=tq, lambda_init=lambda_init),
        grid=(b, n_heads),
        in_specs=[
            pl.BlockSpec(memory_space=pltpu.SMEM),
            pl.BlockSpec((4, HEAD_DIM), lambda bi, h: (0, 0)),
            pl.BlockSpec((None, s, vdim), lambda bi, h: (bi, 0, h)),
            pl.BlockSpec((None, s, vdim), lambda bi, h: (bi, 0, h)),
            pl.BlockSpec((None, s, vdim), lambda bi, h: (bi, 0, n_heads + h)),
            pl.BlockSpec((1, vdim), lambda bi, h: (0, 0)),
        ],
        out_specs=pl.BlockSpec((None, s, vdim), lambda bi, h: (bi, 0, h)),
        out_shape=jax.ShapeDtypeStruct((b, s, n_heads * vdim), BF16),
        compiler_params=_params(("parallel", "parallel")),
        name="diff_attn",
    )(slopes, lam, q, kv, kv, subln_gain.reshape(1, vdim))


def _alibi_slopes(n_heads):
    return 2.0 ** (-8.0 * jnp.arange(1, n_heads + 1, dtype=F32) / n_heads)


def _pad_to(a, axis, size):
    pad = [(0, 0)] * a.ndim
    pad[axis] = (0, size - a.shape[axis])
    return jnp.pad(a, pad)


def _tile(total, preferred):
    return preferred if total % preferred == 0 else total


def kernel(x, norm_gains, ffn_w_in, ffn_w_out, a_w_qkv, a_w_o, kv_norm_gain, b_w_kv, b_w_q, b_lambda,
           b_subln_gain, b_w_o, final_norm_gain):
    b, s, d = x.shape
    m = b * s
    depth = norm_gains.shape[0]
    n_a_layers = a_w_qkv.shape[0]
    d_ff = ffn_w_out.shape[2]
    heads_a = a_w_o.shape[1] // HEAD_DIM
    heads_b = b_w_o.shape[1] // (2 * HEAD_DIM)

    tm = _tile(m, 512)
    tm_proj = _tile(m, 1024)
    tm_ffn = _tile(m, 1024)
    tf = 512
    d_ff_pad = -(-d_ff // tf) * tf

    slopes_a = _alibi_slopes(heads_a)
    slopes_b = _alibi_slopes(heads_b)

    w_in = _pad_to(ffn_w_in.astype(BF16).reshape(depth, 2, d, 2, d_ff), 4, d_ff_pad)
    w_in = w_in.reshape(depth, 2, d, 2 * d_ff_pad)
    w_out = _pad_to(ffn_w_out.astype(BF16), 2, d_ff_pad)

    def ffn(xs, layer, idx, final_gain=None):
        return _ffn(xs, norm_gains[layer, 0 if idx == 0 else 2], w_in, w_out, final_gain,
                    layer=layer, idx=idx, tm=tm_ffn, tf=tf)

    xs = x.reshape(m, d)
    kv = None
    for layer in range(depth):
        xs = ffn(xs, layer, 0)
        if layer < n_a_layers:
            qkv = _norm_matmul(xs, norm_gains[layer, 1], a_w_qkv[layer].astype(BF16), F32,
                               tm=tm_proj, tn=_tile(a_w_qkv.shape[2], 1024))
            merged = _dilated_attn(qkv.reshape(b, s, -1), slopes_a, n_heads=heads_a)
            xs = _matmul_res(merged.reshape(m, -1), a_w_o[layer].astype(BF16), xs, tm=tm)
        else:
            j = layer - n_a_layers
            q = _norm_matmul(xs, norm_gains[layer, 1], b_w_q[j].astype(BF16), BF16,
                             tm=tm_proj, tn=_tile(b_w_q.shape[2], 1024),
                             out_scale=HEAD_DIM ** -0.5 * LOG2_E)
            lambda_init = 0.8 - 0.6 * math.exp(-0.3 * layer)
            o = _diff_attn(q.reshape(b, s, -1), kv, slopes_b, b_lambda[j], b_subln_gain[j],
                           n_heads=heads_b, lambda_init=lambda_init, tq=_tile(s, 256))
            xs = _matmul_res(o.reshape(m, -1), b_w_o[j].astype(BF16), xs, tm=tm)
        last = layer == depth - 1
        xs = ffn(xs, layer, 1, final_norm_gain if last else None)
        if layer == n_a_layers - 1:
            kv = _norm_matmul(xs, kv_norm_gain, b_w_kv.astype(BF16), BF16,
                              tm=tm_proj, tn=_tile(b_w_kv.shape[1], 1024)).reshape(b, s, -1)
    return xs.reshape(b, s, d)
```

```python
import functools
import math

import jax
import jax.numpy as jnp
from jax import lax
from jax.experimental import pallas as pl
from jax.experimental.pallas import tpu as pltpu

F32 = jnp.float32
BF16 = jnp.bfloat16

HEAD_DIM = 128
DILATION_PATTERNS = ((128, 1), (512, 4), (2048, 16))
N_BACK = 128
MACARON_WEIGHT = 0.5
RMS_EPS = 1e-6
SUBLN_EPS = 1e-5
LOG2_E = math.log2(math.e)
SCORE_SCALE_LOG2 = HEAD_DIM ** -0.5 * LOG2_E
MASK_VALUE = -1e30
V7X_VMEM_LIMIT_BYTES = 56 * 1024 * 1024


def _params(semantics):
    return pltpu.CompilerParams(dimension_semantics=semantics,
                                vmem_limit_bytes=V7X_VMEM_LIMIT_BYTES)


def _rms_normed(x, gain, eps):
    y = x * lax.rsqrt(jnp.mean(x * x, axis=-1, keepdims=True) + eps)
    return y * gain


def _dot(a, b):
    return jnp.dot(a, b, preferred_element_type=F32)


def _dot_nt(a, b):
    return lax.dot_general(a, b, (((1,), (1,)), ((), ())), preferred_element_type=F32)


def _pack_w_in_body(w_ref, o_ref, *, d_ff, d_ff_pad):
    zeros = jnp.zeros((o_ref.shape[0], d_ff_pad - d_ff), o_ref.dtype)
    for half in range(2):
        o_ref[:, half * d_ff_pad:half * d_ff_pad + d_ff] = (
            w_ref[:, half * d_ff:(half + 1) * d_ff].astype(o_ref.dtype))
        if d_ff_pad > d_ff:
            o_ref[:, half * d_ff_pad + d_ff:(half + 1) * d_ff_pad] = zeros


def _pack_w_in(w, d_ff_pad, *, tr):
    n, d, two_f = w.shape
    d_ff = two_f // 2
    return pl.pallas_call(
        functools.partial(_pack_w_in_body, d_ff=d_ff, d_ff_pad=d_ff_pad),
        grid=(n, d // tr),
        in_specs=[pl.BlockSpec((None, tr, two_f), lambda i, r: (i, r, 0))],
        out_specs=pl.BlockSpec((None, tr, 2 * d_ff_pad), lambda i, r: (i, r, 0)),
        out_shape=jax.ShapeDtypeStruct((n, d, 2 * d_ff_pad), BF16),
        compiler_params=_params(("parallel", "parallel")),
        name="pack_w_in",
    )(w)


def _pack_w_out_body(w_ref, o_ref, *, d_ff):
    tr = o_ref.shape[0]
    row = pl.program_id(1) * tr + lax.broadcasted_iota(jnp.int32, (tr, 1), 0)
    o_ref[...] = jnp.where(row < d_ff, w_ref[...], 0.0).astype(o_ref.dtype)


def _pack_w_out(w, d_ff_pad, *, tr):
    n, d_ff, d = w.shape
    return pl.pallas_call(
        functools.partial(_pack_w_out_body, d_ff=d_ff),
        grid=(n, d_ff_pad // tr),
        in_specs=[pl.BlockSpec((None, tr, d), lambda i, r: (i, r, 0))],
        out_specs=pl.BlockSpec((None, tr, d), lambda i, r: (i, r, 0)),
        out_shape=jax.ShapeDtypeStruct((n, d_ff_pad, d), BF16),
        compiler_params=_params(("parallel", "parallel")),
        name="pack_w_out",
    )(w)


def _ffn_body(x_ref, g_ref, wg_ref, wu_ref, wo_ref, fg_ref, o_ref, xn_ref, *, final_norm):
    f = pl.program_id(1)

    @pl.when(f == 0)
    def _():
        x = x_ref[...]
        xn_ref[...] = _rms_normed(x, g_ref[...], RMS_EPS).astype(BF16)
        o_ref[...] = x

    xn = xn_ref[...]
    gate = _dot(xn, wg_ref[...])
    up = _dot(xn, wu_ref[...])
    h = (gate * jax.nn.sigmoid(gate) * up * MACARON_WEIGHT).astype(BF16)
    o_ref[...] += _dot(h, wo_ref[...])

    if final_norm:
        @pl.when(f == pl.num_programs(1) - 1)
        def _():
            o_ref[...] = _rms_normed(o_ref[...], fg_ref[...], RMS_EPS)


def _ffn(x2d, gain, w_in, w_out, final_gain, *, layer, idx, tm, tf):
    m, d = x2d.shape
    fp = w_out.shape[1]
    n_f = fp // tf
    wi = 2 * layer + idx
    final_norm = final_gain is not None
    fg = final_gain if final_norm else gain
    return pl.pallas_call(
        functools.partial(_ffn_body, final_norm=final_norm),
        grid=(m // tm, n_f),
        in_specs=[
            pl.BlockSpec((tm, d), lambda i, f: (i, 0)),
            pl.BlockSpec((1, d), lambda i, f: (0, 0)),
            pl.BlockSpec((None, d, tf), lambda i, f: (wi, 0, f)),
            pl.BlockSpec((None, d, tf), lambda i, f: (wi, 0, n_f + f)),
            pl.BlockSpec((None, tf, d), lambda i, f: (wi, f, 0)),
            pl.BlockSpec((1, d), lambda i, f: (0, 0)),
        ],
        out_specs=pl.BlockSpec((tm, d), lambda i, f: (i, 0)),
        out_shape=jax.ShapeDtypeStruct((m, d), F32),
        scratch_shapes=[pltpu.VMEM((tm, d), BF16)],
        compiler_params=_params(("parallel", "arbitrary")),
        name="ffn",
    )(x2d, gain.reshape(1, d), w_in, w_in, w_out, fg.reshape(1, d))


def _norm_matmul_body(x_ref, g_ref, w_ref, o_ref, xn_ref, *, out_scale, scaled_tiles):
    j = pl.program_id(1)

    @pl.when(j == 0)
    def _():
        xn_ref[...] = _rms_normed(x_ref[...], g_ref[...], RMS_EPS).astype(BF16)

    acc = _dot(xn_ref[...], w_ref[...])
    if scaled_tiles:
        acc = acc * jnp.where(j < scaled_tiles, out_scale, 1.0)
    o_ref[...] = acc.astype(o_ref.dtype)


def _norm_matmul(x2d, gain, w, out_dtype, *, tm, tn, out_scale=1.0, scaled_cols=0):
    m, d = x2d.shape
    n = w.shape[1]
    assert scaled_cols % tn == 0
    return pl.pallas_call(
        functools.partial(_norm_matmul_body, out_scale=out_scale, scaled_tiles=scaled_cols // tn),
        grid=(m // tm, n // tn),
        in_specs=[
            pl.BlockSpec((tm, d), lambda i, j: (i, 0)),
            pl.BlockSpec((1, d), lambda i, j: (0, 0)),
            pl.BlockSpec((d, tn), lambda i, j: (0, j)),
        ],
        out_specs=pl.BlockSpec((tm, tn), lambda i, j: (i, j)),
        out_shape=jax.ShapeDtypeStruct((m, n), out_dtype),
        scratch_shapes=[pltpu.VMEM((tm, d), BF16)],
        compiler_params=_params(("parallel", "arbitrary")),
        name="norm_matmul",
    )(x2d, gain.reshape(1, d), w)


def _matmul_res_body(a_ref, w_ref, x_ref, o_ref):
    o_ref[...] = x_ref[...] + _dot(a_ref[...], w_ref[...])


def _matmul_res(a2d, w, x2d, *, tm):
    m, k = a2d.shape
    d = w.shape[1]
    return pl.pallas_call(
        _matmul_res_body,
        grid=(m // tm,),
        in_specs=[
            pl.BlockSpec((tm, k), lambda i: (i, 0)),
            pl.BlockSpec((k, d), lambda i: (0, 0)),
            pl.BlockSpec((tm, d), lambda i: (i, 0)),
        ],
        out_specs=pl.BlockSpec((tm, d), lambda i: (i, 0)),
        out_shape=jax.ShapeDtypeStruct((m, d), F32),
        compiler_params=_params(("parallel",)),
        name="matmul_res",
    )(a2d, w, x2d)


def _bmm(a, b):
    return lax.dot_general(a, b, (((2,), (1,)), ((0,), (0,))), preferred_element_type=F32)


def _bmm_nt(a, b):
    return lax.dot_general(a, b, (((2,), (2,)), ((0,), (0,))), preferred_element_type=F32)


def _residue_rows(c, sub_len, dilation):
    return pl.ds(c, sub_len) if dilation == 1 else pl.ds(c, sub_len, stride=dilation)


def _dilated_attn_body(slopes_ref, *refs, seq):
    n_groups = len(DILATION_PATTERNS)
    qkv_refs = refs[:3 * n_groups]
    o_ref, og_ref, lse_ref = refs[3 * n_groups:]
    slope = slopes_ref[pl.program_id(1)]

    qi = lax.broadcasted_iota(jnp.int32, (N_BACK, N_BACK), 0)
    ki = lax.broadcasted_iota(jnp.int32, (N_BACK, N_BACK), 1)
    steps_cur = (qi - ki).astype(F32)
    steps_prev = steps_cur + float(N_BACK)

    outs, lses = [], []
    for g, (window, dilation) in enumerate(DILATION_PATTERNS):
        q_ref, k_ref, v_ref = qkv_refs[3 * g:3 * g + 3]
        sub_len = seq // dilation
        n_blk = sub_len // N_BACK

        def blocks(ref):
            parts = [ref[_residue_rows(c, sub_len, dilation), :].reshape(n_blk, N_BACK, HEAD_DIM)
                     for c in range(dilation)]
            return parts[0] if dilation == 1 else jnp.concatenate(parts, axis=0)

        def later(a):
            return jnp.concatenate([a[c * n_blk + 1:(c + 1) * n_blk] for c in range(dilation)], axis=0)

        def earlier(a):
            return jnp.concatenate([a[c * n_blk:(c + 1) * n_blk - 1] for c in range(dilation)], axis=0)

        def pad_first(a, fill):
            first = jnp.full((1,) + a.shape[1:], fill, a.dtype)
            parts = []
            for c in range(dilation):
                parts += [first, a[c * (n_blk - 1):(c + 1) * (n_blk - 1)]]
            return jnp.concatenate(parts, axis=0)

        q = blocks(q_ref).astype(BF16)
        k = blocks(k_ref).astype(BF16)
        v = blocks(v_ref).astype(BF16)
        v_ones = jnp.concatenate([v, jnp.ones_like(v)], axis=-1)

        coef = -slope * (float(dilation) * LOG2_E)
        s_cur = _bmm_nt(q, k) + jnp.where(ki <= qi, coef * steps_cur, MASK_VALUE)
        m = jnp.max(s_cur, axis=-1, keepdims=True)
        if n_blk > 1:
            s_prev = _bmm_nt(later(q), earlier(k)) + jnp.where(ki >= qi, coef * steps_prev, MASK_VALUE)
            m = jnp.maximum(m, pad_first(jnp.max(s_prev, axis=-1, keepdims=True), MASK_VALUE))
            p_prev = jnp.exp2(s_prev - later(m)).astype(BF16)
        ol = _bmm(jnp.exp2(s_cur - m).astype(BF16), v_ones)
        if n_blk > 1:
            ol = ol + pad_first(_bmm(p_prev, earlier(v_ones)), 0.0)
        denom = ol[..., HEAD_DIM:]
        o = ol[..., :HEAD_DIM] / denom
        lse = m + jnp.log2(denom)
        if dilation == 1:
            outs.append(o.reshape(seq, HEAD_DIM))
            lses.append(lse.reshape(seq, HEAD_DIM))
        else:
            for c in range(dilation):
                rows = _residue_rows(c, sub_len, dilation)
                og_ref[g - 1, rows, :] = o[c * n_blk:(c + 1) * n_blk].reshape(sub_len, HEAD_DIM)
                lse_ref[g - 1, rows, :] = lse[c * n_blk:(c + 1) * n_blk].reshape(sub_len, HEAD_DIM)
            outs.append(og_ref[g - 1])
            lses.append(lse_ref[g - 1])

    top = functools.reduce(jnp.maximum, lses)
    ws = [jnp.exp2(l - top) for l in lses]
    num = sum(w * o for w, o in zip(ws, outs))
    o_ref[...] = (num / sum(ws)).astype(o_ref.dtype)


def _dilated_attn(qkv, slopes, *, n_heads):
    b, s, _ = qkv.shape
    n_groups = len(DILATION_PATTERNS)

    def col_spec(which, g):
        base = (which * n_groups + g) * n_heads
        return pl.BlockSpec((None, s, HEAD_DIM), lambda bi, h, base=base: (bi, 0, base + h))

    in_specs = [pl.BlockSpec(memory_space=pltpu.SMEM)]
    for g in range(n_groups):
        in_specs += [col_spec(0, g), col_spec(1, g), col_spec(2, g)]
    return pl.pallas_call(
        functools.partial(_dilated_attn_body, seq=s),
        grid=(b, n_heads),
        in_specs=in_specs,
        out_specs=pl.BlockSpec((None, s, HEAD_DIM), lambda bi, h: (bi, 0, h)),
        out_shape=jax.ShapeDtypeStruct((b, s, n_heads * HEAD_DIM), BF16),
        scratch_shapes=[pltpu.VMEM((n_groups - 1, s, HEAD_DIM), F32),
                        pltpu.VMEM((n_groups - 1, s, HEAD_DIM), F32)],
        compiler_params=_params(("parallel", "parallel")),
        name="dilated_attn",
    )(slopes, *([qkv] * (3 * n_groups)))


def _diff_attn_body(slopes_ref, lam_ref, q_ref, k_ref, v_ref, gain_ref, o_ref, *, seq, tq, lambda_init):
    slope = slopes_ref[pl.program_id(1)] * LOG2_E
    vdim = 2 * HEAD_DIM

    lam = lam_ref[...]
    lam_full = (jnp.exp(jnp.sum(lam[0:1] * lam[1:2], axis=-1, keepdims=True))
                - jnp.exp(jnp.sum(lam[2:3] * lam[3:4], axis=-1, keepdims=True))
                + lambda_init)

    key_bias = slope * lax.broadcasted_iota(jnp.int32, (1, seq), 1).astype(F32)
    qi = lax.broadcasted_iota(jnp.int32, (tq, tq), 0)
    ki = lax.broadcasted_iota(jnp.int32, (tq, tq), 1)
    causal = jnp.where(qi >= ki, 0.0, MASK_VALUE)

    for i in range(seq // tq):
        lo, hi = i * tq, (i + 1) * tq
        probs, denoms = [], []
        for cols in (slice(0, HEAD_DIM), slice(HEAD_DIM, vdim)):
            q = q_ref[lo:hi, cols]
            s_diag = _dot_nt(q, k_ref[lo:hi, cols]) + (key_bias[:, lo:hi] + causal)
            m = jnp.max(s_diag, axis=-1, keepdims=True)
            if i > 0:
                s_off = _dot_nt(q, k_ref[0:lo, cols]) + key_bias[:, 0:lo]
                m = jnp.maximum(m, jnp.max(s_off, axis=-1, keepdims=True))
            p_diag = jnp.exp2(s_diag - m)
            l = jnp.sum(p_diag, axis=-1, keepdims=True)
            p = p_diag.astype(BF16)
            if i > 0:
                p_off = jnp.exp2(s_off - m)
                l = l + jnp.sum(p_off, axis=-1, keepdims=True)
                p = jnp.concatenate([p_off.astype(BF16), p], axis=1)
            probs.append(p)
            denoms.append(l)
        pv = _dot(jnp.concatenate(probs, axis=0), v_ref[0:hi, :])
        o = pv[:tq] / denoms[0] - lam_full * (pv[tq:] / denoms[1])
        o = _rms_normed(o, gain_ref[...], SUBLN_EPS) * (1.0 - lambda_init)
        o_ref[lo:hi, :] = o.astype(o_ref.dtype)


def _diff_attn(q, kv, slopes, lam, subln_gain, *, n_heads, lambda_init, tq):
    b, s, _ = q.shape
    vdim = 2 * HEAD_DIM
    return pl.pallas_call(
        functools.partial(_diff_attn_body, seq=s, tq=tq, lambda_init=lambda_init),
        grid=(b, n_heads),
        in_specs=[
            pl.BlockSpec(memory_space=pltpu.SMEM),
            pl.BlockSpec((4, HEAD_DIM), lambda bi, h: (0, 0)),
            pl.BlockSpec((None, s, vdim), lambda bi, h: (bi, 0, h)),
            pl.BlockSpec((None, s, vdim), lambda bi, h: (bi, 0, h)),
            pl.BlockSpec((None, s, vdim), lambda bi, h: (bi, 0, n_heads + h)),
            pl.BlockSpec((1, vdim), lambda bi, h: (0, 0)),
        ],
        out_specs=pl.BlockSpec((None, s, vdim), lambda bi, h: (bi, 0, h)),
        out_shape=jax.ShapeDtypeStruct((b, s, n_heads * vdim), BF16),
        compiler_params=_params(("parallel", "parallel")),
        name="diff_attn",
    )(slopes, lam, q, kv, kv, subln_gain.reshape(1, vdim))


def _alibi_slopes(n_heads):
    return 2.0 ** (-8.0 * jnp.arange(1, n_heads + 1, dtype=F32) / n_heads)


def _tile(total, preferred):
    return preferred if total % preferred == 0 else total


def kernel(x, norm_gains, ffn_w_in, ffn_w_out, a_w_qkv, a_w_o, kv_norm_gain, b_w_kv, b_w_q, b_lambda,
           b_subln_gain, b_w_o, final_norm_gain):
    b, s, d = x.shape
    m = b * s
    depth = norm_gains.shape[0]
    n_a_layers = a_w_qkv.shape[0]
    d_ff = ffn_w_out.shape[2]
    heads_a = a_w_o.shape[1] // HEAD_DIM
    heads_b = b_w_o.shape[1] // (2 * HEAD_DIM)

    tm = _tile(m, 512)
    tm_proj = _tile(m, 1024)
    tm_ffn = _tile(m, 1024)
    tf = 512
    d_ff_pad = -(-d_ff // tf) * tf

    slopes_a = _alibi_slopes(heads_a)
    slopes_b = _alibi_slopes(heads_b)

    w_in = _pack_w_in(ffn_w_in.reshape(2 * depth, d, 2 * d_ff), d_ff_pad, tr=_tile(d, 128))
    w_out = _pack_w_out(ffn_w_out.reshape(2 * depth, d_ff, d), d_ff_pad, tr=tf)

    def ffn(xs, layer, idx, final_gain=None):
        return _ffn(xs, norm_gains[layer, 0 if idx == 0 else 2], w_in, w_out, final_gain,
                    layer=layer, idx=idx, tm=tm_ffn, tf=tf)

    xs = x.reshape(m, d)
    kv = None
    for layer in range(depth):
        xs = ffn(xs, layer, 0)
        if layer < n_a_layers:
            q_cols = a_w_qkv.shape[2] // 3
            qkv = _norm_matmul(xs, norm_gains[layer, 1], a_w_qkv[layer].astype(BF16), F32,
                               tm=tm_proj, tn=_tile(q_cols, 1024),
                               out_scale=SCORE_SCALE_LOG2, scaled_cols=q_cols)
            merged = _dilated_attn(qkv.reshape(b, s, -1), slopes_a, n_heads=heads_a)
            xs = _matmul_res(merged.reshape(m, -1), a_w_o[layer].astype(BF16), xs, tm=tm)
        else:
            j = layer - n_a_layers
            q = _norm_matmul(xs, norm_gains[layer, 1], b_w_q[j].astype(BF16), BF16,
                             tm=tm_proj, tn=_tile(b_w_q.shape[2], 1024),
                             out_scale=SCORE_SCALE_LOG2, scaled_cols=b_w_q.shape[2])
            lambda_init = 0.8 - 0.6 * math.exp(-0.3 * layer)
            o = _diff_attn(q.reshape(b, s, -1), kv, slopes_b, b_lambda[j], b_subln_gain[j],
                           n_heads=heads_b, lambda_init=lambda_init, tq=_tile(s, 256))
            xs = _matmul_res(o.reshape(m, -1), b_w_o[j].astype(BF16), xs, tm=tm)
        last = layer == depth - 1
        xs = ffn(xs, layer, 1, final_norm_gain if last else None)
        if layer == n_a_layers - 1:
            kv = _norm_matmul(xs, kv_norm_gain, b_w_kv.astype(BF16), BF16,
                              tm=tm_proj, tn=_tile(b_w_kv.shape[1], 1024)).reshape(b, s, -1)
    return xs.reshape(b, s, d)
```

```python
import functools
import math

import jax
import jax.numpy as jnp
from jax import lax
from jax.experimental import pallas as pl
from jax.experimental.pallas import tpu as pltpu

F32 = jnp.float32
BF16 = jnp.bfloat16

HEAD_DIM = 128
DILATION_PATTERNS = ((128, 1), (512, 4), (2048, 16))
N_BACK = 128
MACARON_WEIGHT = 0.5
RMS_EPS = 1e-6
SUBLN_EPS = 1e-5
LOG2_E = math.log2(math.e)
SCORE_SCALE_LOG2 = HEAD_DIM ** -0.5 * LOG2_E
MASK_VALUE = -1e30
V7X_VMEM_LIMIT_BYTES = 56 * 1024 * 1024


def _params(semantics):
    return pltpu.CompilerParams(dimension_semantics=semantics,
                                vmem_limit_bytes=V7X_VMEM_LIMIT_BYTES)


def _rms_normed(x, gain, eps):
    y = x * lax.rsqrt(jnp.mean(x * x, axis=-1, keepdims=True) + eps)
    return y * gain


def _dot(a, b):
    return jnp.dot(a, b, preferred_element_type=F32)


def _dot_nt(a, b):
    return lax.dot_general(a, b, (((1,), (1,)), ((), ())), preferred_element_type=F32)


def _pack_w_in_body(w_ref, o_ref, *, d_ff, d_ff_pad):
    zeros = jnp.zeros((o_ref.shape[0], d_ff_pad - d_ff), o_ref.dtype)
    for half in range(2):
        o_ref[:, half * d_ff_pad:half * d_ff_pad + d_ff] = (
            w_ref[:, half * d_ff:(half + 1) * d_ff].astype(o_ref.dtype))
        if d_ff_pad > d_ff:
            o_ref[:, half * d_ff_pad + d_ff:(half + 1) * d_ff_pad] = zeros


def _pack_w_in(w, d_ff_pad, *, tr):
    n, d, two_f = w.shape
    d_ff = two_f // 2
    return pl.pallas_call(
        functools.partial(_pack_w_in_body, d_ff=d_ff, d_ff_pad=d_ff_pad),
        grid=(n, d // tr),
        in_specs=[pl.BlockSpec((None, tr, two_f), lambda i, r: (i, r, 0))],
        out_specs=pl.BlockSpec((None, tr, 2 * d_ff_pad), lambda i, r: (i, r, 0)),
        out_shape=jax.ShapeDtypeStruct((n, d, 2 * d_ff_pad), BF16),
        compiler_params=_params(("parallel", "parallel")),
        name="pack_w_in",
    )(w)


def _pack_w_out_body(w_ref, o_ref, *, d_ff):
    tr = o_ref.shape[0]
    row = pl.program_id(1) * tr + lax.broadcasted_iota(jnp.int32, (tr, 1), 0)
    o_ref[...] = jnp.where(row < d_ff, w_ref[...], 0.0).astype(o_ref.dtype)


def _pack_w_out(w, d_ff_pad, *, tr):
    n, d_ff, d = w.shape
    return pl.pallas_call(
        functools.partial(_pack_w_out_body, d_ff=d_ff),
        grid=(n, d_ff_pad // tr),
        in_specs=[pl.BlockSpec((None, tr, d), lambda i, r: (i, r, 0))],
        out_specs=pl.BlockSpec((None, tr, d), lambda i, r: (i, r, 0)),
        out_shape=jax.ShapeDtypeStruct((n, d_ff_pad, d), BF16),
        compiler_params=_params(("parallel", "parallel")),
        name="pack_w_out",
    )(w)


def _ffn_body(x_ref, g_ref, wg_ref, wu_ref, wo_ref, fg_ref, o_ref, xn_ref, *, final_norm):
    f = pl.program_id(1)

    @pl.when(f == 0)
    def _():
        x = x_ref[...]
        xn_ref[...] = _rms_normed(x, g_ref[...], RMS_EPS).astype(BF16)
        o_ref[...] = x

    xn = xn_ref[...]
    gate = _dot(xn, wg_ref[...])
    up = _dot(xn, wu_ref[...])
    h = (gate * jax.nn.sigmoid(gate) * up * MACARON_WEIGHT).astype(BF16)
    o_ref[...] += _dot(h, wo_ref[...])

    if final_norm:
        @pl.when(f == pl.num_programs(1) - 1)
        def _():
            o_ref[...] = _rms_normed(o_ref[...], fg_ref[...], RMS_EPS)


def _ffn(x2d, gain, w_in, w_out, final_gain, *, layer, idx, tm, tf):
    m, d = x2d.shape
    fp = w_out.shape[1]
    n_f = fp // tf
    wi = 2 * layer + idx
    final_norm = final_gain is not None
    fg = final_gain if final_norm else gain
    return pl.pallas_call(
        functools.partial(_ffn_body, final_norm=final_norm),
        grid=(m // tm, n_f),
        in_specs=[
            pl.BlockSpec((tm, d), lambda i, f: (i, 0)),
            pl.BlockSpec((1, d), lambda i, f: (0, 0)),
            pl.BlockSpec((None, d, tf), lambda i, f: (wi, 0, f)),
            pl.BlockSpec((None, d, tf), lambda i, f: (wi, 0, n_f + f)),
            pl.BlockSpec((None, tf, d), lambda i, f: (wi, f, 0)),
            pl.BlockSpec((1, d), lambda i, f: (0, 0)),
        ],
        out_specs=pl.BlockSpec((tm, d), lambda i, f: (i, 0)),
        out_shape=jax.ShapeDtypeStruct((m, d), F32),
        scratch_shapes=[pltpu.VMEM((tm, d), BF16)],
        compiler_params=_params(("parallel", "arbitrary")),
        name="ffn",
    )(x2d, gain.reshape(1, d), w_in, w_in, w_out, fg.reshape(1, d))


def _norm_matmul_body(x_ref, g_ref, w_ref, o_ref, xn_ref, *, out_scale, scaled_tiles, head_major):
    j = pl.program_id(1)

    @pl.when(j == 0)
    def _():
        xn_ref[...] = _rms_normed(x_ref[...], g_ref[...], RMS_EPS).astype(BF16)

    acc = _dot(xn_ref[...], w_ref[...])
    if scaled_tiles:
        acc = acc * jnp.where(j < scaled_tiles, out_scale, 1.0)
    if head_major:
        for hh in range(o_ref.shape[0]):
            o_ref[hh] = acc[:, hh * HEAD_DIM:(hh + 1) * HEAD_DIM].astype(o_ref.dtype)
    else:
        o_ref[...] = acc.astype(o_ref.dtype)


def _norm_matmul(x2d, gain, w, out_dtype, *, tm, tn, out_scale=1.0, scaled_cols=0, head_major=False):
    m, d = x2d.shape
    n = w.shape[1]
    assert scaled_cols % tn == 0
    if head_major:
        out_spec = pl.BlockSpec((tn // HEAD_DIM, tm, HEAD_DIM), lambda i, j: (j, i, 0))
        out_shape = jax.ShapeDtypeStruct((n // HEAD_DIM, m, HEAD_DIM), out_dtype)
    else:
        out_spec = pl.BlockSpec((tm, tn), lambda i, j: (i, j))
        out_shape = jax.ShapeDtypeStruct((m, n), out_dtype)
    return pl.pallas_call(
        functools.partial(_norm_matmul_body, out_scale=out_scale, scaled_tiles=scaled_cols // tn,
                          head_major=head_major),
        grid=(m // tm, n // tn),
        in_specs=[
            pl.BlockSpec((tm, d), lambda i, j: (i, 0)),
            pl.BlockSpec((1, d), lambda i, j: (0, 0)),
            pl.BlockSpec((d, tn), lambda i, j: (0, j)),
        ],
        out_specs=out_spec,
        out_shape=out_shape,
        scratch_shapes=[pltpu.VMEM((tm, d), BF16)],
        compiler_params=_params(("parallel", "arbitrary")),
        name="norm_matmul",
    )(x2d, gain.reshape(1, d), w)


def _matmul_res_body(a_ref, w_ref, x_ref, o_ref):
    o_ref[...] = x_ref[...] + _dot(a_ref[...], w_ref[...])


def _matmul_res(a2d, w, x2d, *, tm):
    m, k = a2d.shape
    d = w.shape[1]
    return pl.pallas_call(
        _matmul_res_body,
        grid=(m // tm,),
        in_specs=[
            pl.BlockSpec((tm, k), lambda i: (i, 0)),
            pl.BlockSpec((k, d), lambda i: (0, 0)),
            pl.BlockSpec((tm, d), lambda i: (i, 0)),
        ],
        out_specs=pl.BlockSpec((tm, d), lambda i: (i, 0)),
        out_shape=jax.ShapeDtypeStruct((m, d), F32),
        compiler_params=_params(("parallel",)),
        name="matmul_res",
    )(a2d, w, x2d)


def _bmm(a, b):
    return lax.dot_general(a, b, (((2,), (1,)), ((0,), (0,))), preferred_element_type=F32)


def _bmm_nt(a, b):
    return lax.dot_general(a, b, (((2,), (2,)), ((0,), (0,))), preferred_element_type=F32)


def _residue_rows(c, sub_len, dilation):
    return pl.ds(c, sub_len) if dilation == 1 else pl.ds(c, sub_len, stride=dilation)


def _dilated_attn_body(slopes_ref, *refs, seq):
    n_groups = len(DILATION_PATTERNS)
    qkv_refs = refs[:3 * n_groups]
    o_ref, og_ref, lse_ref = refs[3 * n_groups:]
    slope = slopes_ref[pl.program_id(1)]

    qi = lax.broadcasted_iota(jnp.int32, (N_BACK, N_BACK), 0)
    ki = lax.broadcasted_iota(jnp.int32, (N_BACK, N_BACK), 1)
    steps_cur = (qi - ki).astype(F32)
    steps_prev = steps_cur + float(N_BACK)

    outs, lses = [], []
    for g, (window, dilation) in enumerate(DILATION_PATTERNS):
        q_ref, k_ref, v_ref = qkv_refs[3 * g:3 * g + 3]
        sub_len = seq // dilation
        n_blk = sub_len // N_BACK

        def blocks(ref):
            parts = [ref[_residue_rows(c, sub_len, dilation), :].reshape(n_blk, N_BACK, HEAD_DIM)
                     for c in range(dilation)]
            return parts[0] if dilation == 1 else jnp.concatenate(parts, axis=0)

        def later(a):
            return jnp.concatenate([a[c * n_blk + 1:(c + 1) * n_blk] for c in range(dilation)], axis=0)

        def earlier(a):
            return jnp.concatenate([a[c * n_blk:(c + 1) * n_blk - 1] for c in range(dilation)], axis=0)

        def pad_first(a, fill):
            first = jnp.full((1,) + a.shape[1:], fill, a.dtype)
            parts = []
            for c in range(dilation):
                parts += [first, a[c * (n_blk - 1):(c + 1) * (n_blk - 1)]]
            return jnp.concatenate(parts, axis=0)

        q = blocks(q_ref).astype(BF16)
        k = blocks(k_ref).astype(BF16)
        v = blocks(v_ref).astype(BF16)
        v_ones = jnp.concatenate([v, jnp.ones_like(v)], axis=-1)

        coef = -slope * (float(dilation) * LOG2_E)
        s_cur = _bmm_nt(q, k) + jnp.where(ki <= qi, coef * steps_cur, MASK_VALUE)
        m = jnp.max(s_cur, axis=-1, keepdims=True)
        if n_blk > 1:
            s_prev = _bmm_nt(later(q), earlier(k)) + jnp.where(ki >= qi, coef * steps_prev, MASK_VALUE)
            m = jnp.maximum(m, pad_first(jnp.max(s_prev, axis=-1, keepdims=True), MASK_VALUE))
            p_prev = jnp.exp2(s_prev - later(m)).astype(BF16)
        ol = _bmm(jnp.exp2(s_cur - m).astype(BF16), v_ones)
        if n_blk > 1:
            ol = ol + pad_first(_bmm(p_prev, earlier(v_ones)), 0.0)
        denom = ol[..., HEAD_DIM:]
        o = ol[..., :HEAD_DIM] / denom
        lse = m + jnp.log2(denom)
        if dilation == 1:
            outs.append(o.reshape(seq, HEAD_DIM))
            lses.append(lse.reshape(seq, HEAD_DIM))
        else:
            for c in range(dilation):
                rows = _residue_rows(c, sub_len, dilation)
                og_ref[g - 1, rows, :] = o[c * n_blk:(c + 1) * n_blk].reshape(sub_len, HEAD_DIM)
                lse_ref[g - 1, rows, :] = lse[c * n_blk:(c + 1) * n_blk].reshape(sub_len, HEAD_DIM)
            outs.append(og_ref[g - 1])
            lses.append(lse_ref[g - 1])

    top = functools.reduce(jnp.maximum, lses)
    ws = [jnp.exp2(l - top) for l in lses]
    num = sum(w * o for w, o in zip(ws, outs))
    o_ref[...] = (num / sum(ws)).astype(o_ref.dtype)


def _dilated_attn(qkv, slopes, *, batch, n_heads):
    s = qkv.shape[1] // batch
    n_groups = len(DILATION_PATTERNS)

    def slab_spec(which, g):
        base = (which * n_groups + g) * n_heads
        return pl.BlockSpec((None, s, HEAD_DIM), lambda bi, h, base=base: (base + h, bi, 0))

    in_specs = [pl.BlockSpec(memory_space=pltpu.SMEM)]
    for g in range(n_groups):
        in_specs += [slab_spec(0, g), slab_spec(1, g), slab_spec(2, g)]
    return pl.pallas_call(
        functools.partial(_dilated_attn_body, seq=s),
        grid=(batch, n_heads),
        in_specs=in_specs,
        out_specs=pl.BlockSpec((None, s, HEAD_DIM), lambda bi, h: (bi, 0, h)),
        out_shape=jax.ShapeDtypeStruct((batch, s, n_heads * HEAD_DIM), BF16),
        scratch_shapes=[pltpu.VMEM((n_groups - 1, s, HEAD_DIM), F32),
                        pltpu.VMEM((n_groups - 1, s, HEAD_DIM), F32)],
        compiler_params=_params(("parallel", "parallel")),
        name="dilated_attn",
    )(slopes, *([qkv] * (3 * n_groups)))


def _diff_attn_body(slopes_ref, lam_ref, q_ref, k_ref, v_ref, gain_ref, o_ref, *, seq, tq, lambda_init):
    slope = slopes_ref[pl.program_id(1)] * LOG2_E
    vdim = 2 * HEAD_DIM

    lam = lam_ref[...]
    lam_full = (jnp.exp(jnp.sum(lam[0:1] * lam[1:2], axis=-1, keepdims=True))
                - jnp.exp(jnp.sum(lam[2:3] * lam[3:4], axis=-1, keepdims=True))
                + lambda_init)

    key_bias = slope * lax.broadcasted_iota(jnp.int32, (1, seq), 1).astype(F32)
    qi = lax.broadcasted_iota(jnp.int32, (tq, tq), 0)
    ki = lax.broadcasted_iota(jnp.int32, (tq, tq), 1)
    causal = jnp.where(qi >= ki, 0.0, MASK_VALUE)

    for i in range(seq // tq):
        lo, hi = i * tq, (i + 1) * tq
        probs, denoms = [], []
        for cols in (slice(0, HEAD_DIM), slice(HEAD_DIM, vdim)):
            q = q_ref[lo:hi, cols]
            tiles = []
            m = None
            for j in range(i + 1):
                klo, khi = j * tq, (j + 1) * tq
                bias = key_bias[:, klo:khi] + causal if j == i else key_bias[:, klo:khi]
                s = _dot_nt(q, k_ref[klo:khi, cols]) + bias
                tile_max = jnp.max(s, axis=-1, keepdims=True)
                m = tile_max if m is None else jnp.maximum(m, tile_max)
                tiles.append(s)
            l = None
            ps = []
            for s in tiles:
                p = jnp.exp2(s - m)
                tile_sum = jnp.sum(p, axis=-1, keepdims=True)
                l = tile_sum if l is None else l + tile_sum
                ps.append(p.astype(BF16))
            probs.append(ps[0] if len(ps) == 1 else jnp.concatenate(ps, axis=1))
            denoms.append(l)
        pv = _dot(jnp.concatenate(probs, axis=0), v_ref[0:hi, :])
        o = pv[:tq] / denoms[0] - lam_full * (pv[tq:] / denoms[1])
        o = _rms_normed(o, gain_ref[...], SUBLN_EPS) * (1.0 - lambda_init)
        o_ref[lo:hi, :] = o.astype(o_ref.dtype)


def _diff_attn(q, kv, slopes, lam, subln_gain, *, n_heads, lambda_init, tq):
    b, s, _ = q.shape
    vdim = 2 * HEAD_DIM
    return pl.pallas_call(
        functools.partial(_diff_attn_body, seq=s, tq=tq, lambda_init=lambda_init),
        grid=(b, n_heads),
        in_specs=[
            pl.BlockSpec(memory_space=pltpu.SMEM),
            pl.BlockSpec((4, HEAD_DIM), lambda bi, h: (0, 0)),
            pl.BlockSpec((None, s, vdim), lambda bi, h: (bi, 0, h)),
            pl.BlockSpec((None, s, vdim), lambda bi, h: (bi, 0, h)),
            pl.BlockSpec((None, s, vdim), lambda bi, h: (bi, 0, n_heads + h)),
            pl.BlockSpec((1, vdim), lambda bi, h: (0, 0)),
        ],
        out_specs=pl.BlockSpec((None, s, vdim), lambda bi, h: (bi, 0, h)),
        out_shape=jax.ShapeDtypeStruct((b, s, n_heads * vdim), BF16),
        compiler_params=_params(("parallel", "parallel")),
        name="diff_attn",
    )(slopes, lam, q, kv, kv, subln_gain.reshape(1, vdim))


def _alibi_slopes(n_heads):
    return 2.0 ** (-8.0 * jnp.arange(1, n_heads + 1, dtype=F32) / n_heads)


def _tile(total, preferred):
    return preferred if total % preferred == 0 else total


def kernel(x, norm_gains, ffn_w_in, ffn_w_out, a_w_qkv, a_w_o, kv_norm_gain, b_w_kv, b_w_q, b_lambda,
           b_subln_gain, b_w_o, final_norm_gain):
    b, s, d = x.shape
    m = b * s
    depth = norm_gains.shape[0]
    n_a_layers = a_w_qkv.shape[0]
    d_ff = ffn_w_out.shape[2]
    heads_a = a_w_o.shape[1] // HEAD_DIM
    heads_b = b_w_o.shape[1] // (2 * HEAD_DIM)

    tm = _tile(m, 512)
    tm_proj = _tile(m, 1024)
    tm_ffn = _tile(m, 1024)
    tf = 512
    d_ff_pad = -(-d_ff // tf) * tf

    slopes_a = _alibi_slopes(heads_a)
    slopes_b = _alibi_slopes(heads_b)

    w_in = _pack_w_in(ffn_w_in.reshape(2 * depth, d, 2 * d_ff), d_ff_pad, tr=_tile(d, 128))
    w_out = _pack_w_out(ffn_w_out.reshape(2 * depth, d_ff, d), d_ff_pad, tr=tf)

    def ffn(xs, layer, idx, final_gain=None):
        return _ffn(xs, norm_gains[layer, 0 if idx == 0 else 2], w_in, w_out, final_gain,
                    layer=layer, idx=idx, tm=tm_ffn, tf=tf)

    xs = x.reshape(m, d)
    kv = None
    for layer in range(depth):
        xs = ffn(xs, layer, 0)
        if layer < n_a_layers:
            q_cols = a_w_qkv.shape[2] // 3
            qkv = _norm_matmul(xs, norm_gains[layer, 1], a_w_qkv[layer].astype(BF16), F32,
                               tm=tm_proj, tn=_tile(q_cols, 1024),
                               out_scale=SCORE_SCALE_LOG2, scaled_cols=q_cols, head_major=True)
            merged = _dilated_attn(qkv, slopes_a, batch=b, n_heads=heads_a)
            xs = _matmul_res(merged.reshape(m, -1), a_w_o[layer].astype(BF16), xs, tm=tm)
        else:
            j = layer - n_a_layers
            q = _norm_matmul(xs, norm_gains[layer, 1], b_w_q[j].astype(BF16), BF16,
                             tm=tm_proj, tn=_tile(b_w_q.shape[2], 1024),
                             out_scale=SCORE_SCALE_LOG2, scaled_cols=b_w_q.shape[2])
            lambda_init = 0.8 - 0.6 * math.exp(-0.3 * layer)
            o = _diff_attn(q.reshape(b, s, -1), kv, slopes_b, b_lambda[j], b_subln_gain[j],
                           n_heads=heads_b, lambda_init=lambda_init, tq=_tile(s, 256))
            xs = _matmul_res(o.reshape(m, -1), b_w_o[j].astype(BF16), xs, tm=tm)
        last = layer == depth - 1
        xs = ffn(xs, layer, 1, final_norm_gain if last else None)
        if layer == n_a_layers - 1:
            kv = _norm_matmul(xs, kv_norm_gain, b_w_kv.astype(BF16), BF16,
                              tm=tm_proj, tn=_tile(b_w_kv.shape[1], 1024)).reshape(b, s, -1)
    return xs.reshape(b, s, d)
```

```python
import functools
import math

import jax
import jax.numpy as jnp
from jax import lax
from jax.experimental import pallas as pl
from jax.experimental.pallas import tpu as pltpu

F32 = jnp.float32
BF16 = jnp.bfloat16

HEAD_DIM = 128
DILATION_PATTERNS = ((128, 1), (512, 4), (2048, 16))
N_BACK = 128
MACARON_WEIGHT = 0.5
RMS_EPS = 1e-6
SUBLN_EPS = 1e-5
LOG2_E = math.log2(math.e)
SCORE_SCALE_LOG2 = HEAD_DIM ** -0.5 * LOG2_E
MASK_VALUE = -1e30
V7X_VMEM_LIMIT_BYTES = 56 * 1024 * 1024


def _params(semantics):
    return pltpu.CompilerParams(dimension_semantics=semantics,
                                vmem_limit_bytes=V7X_VMEM_LIMIT_BYTES)


def _rms_normed(x, gain, eps):
    y = x * lax.rsqrt(jnp.mean(x * x, axis=-1, keepdims=True) + eps)
    return y * gain


def _dot(a, b):
    return jnp.dot(a, b, preferred_element_type=F32)


def _dot_nt(a, b):
    return lax.dot_general(a, b, (((1,), (1,)), ((), ())), preferred_element_type=F32)


def _pack_w_in_body(w_ref, o_ref, *, d_ff, d_ff_pad):
    zeros = jnp.zeros((o_ref.shape[0], d_ff_pad - d_ff), o_ref.dtype)
    for half in range(2):
        o_ref[:, half * d_ff_pad:half * d_ff_pad + d_ff] = (
            w_ref[:, half * d_ff:(half + 1) * d_ff].astype(o_ref.dtype))
        if d_ff_pad > d_ff:
            o_ref[:, half * d_ff_pad + d_ff:(half + 1) * d_ff_pad] = zeros


def _pack_w_in(w, d_ff_pad, *, tr):
    n, d, two_f = w.shape
    d_ff = two_f // 2
    return pl.pallas_call(
        functools.partial(_pack_w_in_body, d_ff=d_ff, d_ff_pad=d_ff_pad),
        grid=(n, d // tr),
        in_specs=[pl.BlockSpec((None, tr, two_f), lambda i, r: (i, r, 0))],
        out_specs=pl.BlockSpec((None, tr, 2 * d_ff_pad), lambda i, r: (i, r, 0)),
        out_shape=jax.ShapeDtypeStruct((n, d, 2 * d_ff_pad), BF16),
        compiler_params=_params(("parallel", "parallel")),
        name="pack_w_in",
    )(w)


def _pack_w_out_body(w_ref, o_ref, *, d_ff):
    tr = o_ref.shape[0]
    row = pl.program_id(1) * tr + lax.broadcasted_iota(jnp.int32, (tr, 1), 0)
    o_ref[...] = jnp.where(row < d_ff, w_ref[...], 0.0).astype(o_ref.dtype)


def _pack_w_out(w, d_ff_pad, *, tr):
    n, d_ff, d = w.shape
    return pl.pallas_call(
        functools.partial(_pack_w_out_body, d_ff=d_ff),
        grid=(n, d_ff_pad // tr),
        in_specs=[pl.BlockSpec((None, tr, d), lambda i, r: (i, r, 0))],
        out_specs=pl.BlockSpec((None, tr, d), lambda i, r: (i, r, 0)),
        out_shape=jax.ShapeDtypeStruct((n, d_ff_pad, d), BF16),
        compiler_params=_params(("parallel", "parallel")),
        name="pack_w_out",
    )(w)


def _ffn_body(x_ref, g_ref, wg_ref, wu_ref, wo_ref, fg_ref, o_ref, xn_ref, *, final_norm):
    f = pl.program_id(1)

    @pl.when(f == 0)
    def _():
        x = x_ref[...]
        xn_ref[...] = _rms_normed(x, g_ref[...], RMS_EPS).astype(BF16)
        o_ref[...] = x

    xn = xn_ref[...]
    gate = _dot(xn, wg_ref[...])
    up = _dot(xn, wu_ref[...])
    h = (gate * jax.nn.sigmoid(gate) * up * MACARON_WEIGHT).astype(BF16)
    o_ref[...] += _dot(h, wo_ref[...])

    if final_norm:
        @pl.when(f == pl.num_programs(1) - 1)
        def _():
            o_ref[...] = _rms_normed(o_ref[...], fg_ref[...], RMS_EPS)


def _ffn(x2d, gain, w_in, w_out, final_gain, *, layer, idx, tm, tf):
    m, d = x2d.shape
    fp = w_out.shape[1]
    n_f = fp // tf
    wi = 2 * layer + idx
    final_norm = final_gain is not None
    fg = final_gain if final_norm else gain
    return pl.pallas_call(
        functools.partial(_ffn_body, final_norm=final_norm),
        grid=(m // tm, n_f),
        in_specs=[
            pl.BlockSpec((tm, d), lambda i, f: (i, 0)),
            pl.BlockSpec((1, d), lambda i, f: (0, 0)),
            pl.BlockSpec((None, d, tf), lambda i, f: (wi, 0, f)),
            pl.BlockSpec((None, d, tf), lambda i, f: (wi, 0, n_f + f)),
            pl.BlockSpec((None, tf, d), lambda i, f: (wi, f, 0)),
            pl.BlockSpec((1, d), lambda i, f: (0, 0)),
        ],
        out_specs=pl.BlockSpec((tm, d), lambda i, f: (i, 0)),
        out_shape=jax.ShapeDtypeStruct((m, d), F32),
        scratch_shapes=[pltpu.VMEM((tm, d), BF16)],
        compiler_params=_params(("parallel", "arbitrary")),
        name="ffn",
    )(x2d, gain.reshape(1, d), w_in, w_in, w_out, fg.reshape(1, d))


def _norm_matmul_body(x_ref, g_ref, w_ref, o_ref, xn_ref, *, out_scale, scaled_tiles, head_major):
    j = pl.program_id(1)

    @pl.when(j == 0)
    def _():
        xn_ref[...] = _rms_normed(x_ref[...], g_ref[...], RMS_EPS).astype(BF16)

    acc = _dot(xn_ref[...], w_ref[...])
    if scaled_tiles:
        acc = acc * jnp.where(j < scaled_tiles, out_scale, 1.0)
    if head_major:
        for hh in range(o_ref.shape[0]):
            o_ref[hh] = acc[:, hh * HEAD_DIM:(hh + 1) * HEAD_DIM].astype(o_ref.dtype)
    else:
        o_ref[...] = acc.astype(o_ref.dtype)


def _norm_matmul(x2d, gain, w, out_dtype, *, tm, tn, out_scale=1.0, scaled_cols=0, head_major=False):
    m, d = x2d.shape
    n = w.shape[1]
    assert scaled_cols % tn == 0
    if head_major:
        out_spec = pl.BlockSpec((tn // HEAD_DIM, tm, HEAD_DIM), lambda i, j: (j, i, 0))
        out_shape = jax.ShapeDtypeStruct((n // HEAD_DIM, m, HEAD_DIM), out_dtype)
    else:
        out_spec = pl.BlockSpec((tm, tn), lambda i, j: (i, j))
        out_shape = jax.ShapeDtypeStruct((m, n), out_dtype)
    return pl.pallas_call(
        functools.partial(_norm_matmul_body, out_scale=out_scale, scaled_tiles=scaled_cols // tn,
                          head_major=head_major),
        grid=(m // tm, n // tn),
        in_specs=[
            pl.BlockSpec((tm, d), lambda i, j: (i, 0)),
            pl.BlockSpec((1, d), lambda i, j: (0, 0)),
            pl.BlockSpec((d, tn), lambda i, j: (0, j)),
        ],
        out_specs=out_spec,
        out_shape=out_shape,
        scratch_shapes=[pltpu.VMEM((tm, d), BF16)],
        compiler_params=_params(("parallel", "arbitrary")),
        name="norm_matmul",
    )(x2d, gain.reshape(1, d), w)


def _matmul_res_body(a_ref, w_ref, x_ref, o_ref):
    o_ref[...] = x_ref[...] + _dot(a_ref[...], w_ref[...])


def _matmul_res(a2d, w, x2d, *, tm):
    m, k = a2d.shape
    d = w.shape[1]
    return pl.pallas_call(
        _matmul_res_body,
        grid=(m // tm,),
        in_specs=[
            pl.BlockSpec((tm, k), lambda i: (i, 0)),
            pl.BlockSpec((k, d), lambda i: (0, 0), pipeline_mode=pl.Buffered(1)),
            pl.BlockSpec((tm, d), lambda i: (i, 0)),
        ],
        out_specs=pl.BlockSpec((tm, d), lambda i: (i, 0)),
        out_shape=jax.ShapeDtypeStruct((m, d), F32),
        compiler_params=_params(("parallel",)),
        name="matmul_res",
    )(a2d, w, x2d)


def _bmm(a, b):
    return lax.dot_general(a, b, (((2,), (1,)), ((0,), (0,))), preferred_element_type=F32)


def _bmm_nt(a, b):
    return lax.dot_general(a, b, (((2,), (2,)), ((0,), (0,))), preferred_element_type=F32)


def _residue_rows(c, sub_len, dilation):
    return pl.ds(c, sub_len) if dilation == 1 else pl.ds(c, sub_len, stride=dilation)


def _dilated_attn_body(slopes_ref, *refs, seq):
    n_groups = len(DILATION_PATTERNS)
    qkv_refs = refs[:3 * n_groups]
    o_ref, og_ref, lse_ref = refs[3 * n_groups:]
    slope = slopes_ref[pl.program_id(1)]

    qi = lax.broadcasted_iota(jnp.int32, (N_BACK, N_BACK), 0)
    ki = lax.broadcasted_iota(jnp.int32, (N_BACK, N_BACK), 1)
    steps_cur = (qi - ki).astype(F32)
    steps_prev = steps_cur + float(N_BACK)

    outs, lses = [], []
    for g, (window, dilation) in enumerate(DILATION_PATTERNS):
        q_ref, k_ref, v_ref = qkv_refs[3 * g:3 * g + 3]
        sub_len = seq // dilation
        n_blk = sub_len // N_BACK

        def blocks(ref):
            parts = [ref[_residue_rows(c, sub_len, dilation), :].reshape(n_blk, N_BACK, HEAD_DIM)
                     for c in range(dilation)]
            return parts[0] if dilation == 1 else jnp.concatenate(parts, axis=0)

        def later(a):
            return jnp.concatenate([a[c * n_blk + 1:(c + 1) * n_blk] for c in range(dilation)], axis=0)

        def earlier(a):
            return jnp.concatenate([a[c * n_blk:(c + 1) * n_blk - 1] for c in range(dilation)], axis=0)

        def pad_first(a, fill):
            first = jnp.full((1,) + a.shape[1:], fill, a.dtype)
            parts = []
            for c in range(dilation):
                parts += [first, a[c * (n_blk - 1):(c + 1) * (n_blk - 1)]]
            return jnp.concatenate(parts, axis=0)

        q = blocks(q_ref).astype(BF16)
        k = blocks(k_ref).astype(BF16)
        v = blocks(v_ref).astype(BF16)
        v_ones = jnp.concatenate([v, jnp.ones_like(v)], axis=-1)

        coef = -slope * (float(dilation) * LOG2_E)
        s_cur = _bmm_nt(q, k) + jnp.where(ki <= qi, coef * steps_cur, MASK_VALUE)
        m = jnp.max(s_cur, axis=-1, keepdims=True)
        if n_blk > 1:
            s_prev = _bmm_nt(later(q), earlier(k)) + jnp.where(ki >= qi, coef * steps_prev, MASK_VALUE)
            m = jnp.maximum(m, pad_first(jnp.max(s_prev, axis=-1, keepdims=True), MASK_VALUE))
            p_prev = jnp.exp2(s_prev - later(m)).astype(BF16)
        ol = _bmm(jnp.exp2(s_cur - m).astype(BF16), v_ones)
        if n_blk > 1:
            ol = ol + pad_first(_bmm(p_prev, earlier(v_ones)), 0.0)
        denom = ol[..., HEAD_DIM:]
        o = ol[..., :HEAD_DIM] / denom
        lse = m + jnp.log2(denom)
        if dilation == 1:
            outs.append(o.reshape(seq, HEAD_DIM))
            lses.append(lse.reshape(seq, HEAD_DIM))
        else:
            for c in range(dilation):
                rows = _residue_rows(c, sub_len, dilation)
                og_ref[g - 1, rows, :] = o[c * n_blk:(c + 1) * n_blk].reshape(sub_len, HEAD_DIM)
                lse_ref[g - 1, rows, :] = lse[c * n_blk:(c + 1) * n_blk].reshape(sub_len, HEAD_DIM)
            outs.append(og_ref[g - 1])
            lses.append(lse_ref[g - 1])

    top = functools.reduce(jnp.maximum, lses)
    ws = [jnp.exp2(l - top) for l in lses]
    num = sum(w * o for w, o in zip(ws, outs))
    o_ref[...] = (num / sum(ws)).astype(o_ref.dtype)


def _dilated_attn(qkv, slopes, *, batch, n_heads):
    s = qkv.shape[1] // batch
    n_groups = len(DILATION_PATTERNS)

    def slab_spec(which, g):
        base = (which * n_groups + g) * n_heads
        return pl.BlockSpec((None, s, HEAD_DIM), lambda bi, h, base=base: (base + h, bi, 0))

    in_specs = [pl.BlockSpec(memory_space=pltpu.SMEM)]
    for g in range(n_groups):
        in_specs += [slab_spec(0, g), slab_spec(1, g), slab_spec(2, g)]
    return pl.pallas_call(
        functools.partial(_dilated_attn_body, seq=s),
        grid=(batch, n_heads),
        in_specs=in_specs,
        out_specs=pl.BlockSpec((None, s, HEAD_DIM), lambda bi, h: (bi, 0, h)),
        out_shape=jax.ShapeDtypeStruct((batch, s, n_heads * HEAD_DIM), BF16),
        scratch_shapes=[pltpu.VMEM((n_groups - 1, s, HEAD_DIM), F32),
                        pltpu.VMEM((n_groups - 1, s, HEAD_DIM), F32)],
        compiler_params=_params(("parallel", "parallel")),
        name="dilated_attn",
    )(slopes, *([qkv] * (3 * n_groups)))


def _diff_attn_body(slopes_ref, lam_ref, q_ref, k_ref, v_ref, gain_ref, o_ref, *, seq, tq, lambda_init):
    slope = slopes_ref[pl.program_id(1)] * LOG2_E
    vdim = 2 * HEAD_DIM

    lam = lam_ref[...]
    lam_full = (jnp.exp(jnp.sum(lam[0:1] * lam[1:2], axis=-1, keepdims=True))
                - jnp.exp(jnp.sum(lam[2:3] * lam[3:4], axis=-1, keepdims=True))
                + lambda_init)

    key_bias = slope * lax.broadcasted_iota(jnp.int32, (1, seq), 1).astype(F32)
    qi = lax.broadcasted_iota(jnp.int32, (tq, tq), 0)
    ki = lax.broadcasted_iota(jnp.int32, (tq, tq), 1)
    causal = jnp.where(qi >= ki, 0.0, MASK_VALUE)

    for i in range(seq // tq):
        lo, hi = i * tq, (i + 1) * tq
        probs, denoms = [], []
        for cols in (slice(0, HEAD_DIM), slice(HEAD_DIM, vdim)):
            q = q_ref[lo:hi, cols]
            tiles = []
            m = None
            for j in range(i + 1):
                klo, khi = j * tq, (j + 1) * tq
                bias = key_bias[:, klo:khi] + causal if j == i else key_bias[:, klo:khi]
                s = _dot_nt(q, k_ref[klo:khi, cols]) + bias
                tile_max = jnp.max(s, axis=-1, keepdims=True)
                m = tile_max if m is None else jnp.maximum(m, tile_max)
                tiles.append(s)
            l = None
            ps = []
            for s in tiles:
                p = jnp.exp2(s - m)
                tile_sum = jnp.sum(p, axis=-1, keepdims=True)
                l = tile_sum if l is None else l + tile_sum
                ps.append(p.astype(BF16))
            probs.append(ps[0] if len(ps) == 1 else jnp.concatenate(ps, axis=1))
            denoms.append(l)
        pv = _dot(jnp.concatenate(probs, axis=0), v_ref[0:hi, :])
        o = pv[:tq] / denoms[0] - lam_full * (pv[tq:] / denoms[1])
        o = _rms_normed(o, gain_ref[...], SUBLN_EPS) * (1.0 - lambda_init)
        o_ref[lo:hi, :] = o.astype(o_ref.dtype)


def _diff_attn(q, kv, slopes, lam, subln_gain, *, n_heads, lambda_init, tq):
    b, s, _ = q.shape
    vdim = 2 * HEAD_DIM
    return pl.pallas_call(
        functools.partial(_diff_attn_body, seq=s, tq=tq, lambda_init=lambda_init),
        grid=(b, n_heads),
        in_specs=[
            pl.BlockSpec(memory_space=pltpu.SMEM),
            pl.BlockSpec((4, HEAD_DIM), lambda bi, h: (0, 0)),
            pl.BlockSpec((None, s, vdim), lambda bi, h: (bi, 0, h)),
            pl.BlockSpec((None, s, vdim), lambda bi, h: (bi, 0, h)),
            pl.BlockSpec((None, s, vdim), lambda bi, h: (bi, 0, n_heads + h)),
            pl.BlockSpec((1, vdim), lambda bi, h: (0, 0)),
        ],
        out_specs=pl.BlockSpec((None, s, vdim), lambda bi, h: (bi, 0, h)),
        out_shape=jax.ShapeDtypeStruct((b, s, n_heads * vdim), BF16),
        compiler_params=_params(("parallel", "parallel")),
        name="diff_attn",
    )(slopes, lam, q, kv, kv, subln_gain.reshape(1, vdim))


def _alibi_slopes(n_heads):
    return 2.0 ** (-8.0 * jnp.arange(1, n_heads + 1, dtype=F32) / n_heads)


def _tile(total, preferred):
    return preferred if total % preferred == 0 else total


def kernel(x, norm_gains, ffn_w_in, ffn_w_out, a_w_qkv, a_w_o, kv_norm_gain, b_w_kv, b_w_q, b_lambda,
           b_subln_gain, b_w_o, final_norm_gain):
    b, s, d = x.shape
    m = b * s
    depth = norm_gains.shape[0]
    n_a_layers = a_w_qkv.shape[0]
    d_ff = ffn_w_out.shape[2]
    heads_a = a_w_o.shape[1] // HEAD_DIM
    heads_b = b_w_o.shape[1] // (2 * HEAD_DIM)

    tm_proj = _tile(m, 1024)
    tm_ffn = _tile(m, 1024)
    tf = 512
    d_ff_pad = -(-d_ff // tf) * tf

    slopes_a = _alibi_slopes(heads_a)
    slopes_b = _alibi_slopes(heads_b)

    w_in = _pack_w_in(ffn_w_in.reshape(2 * depth, d, 2 * d_ff), d_ff_pad, tr=_tile(d, 128))
    w_out = _pack_w_out(ffn_w_out.reshape(2 * depth, d_ff, d), d_ff_pad, tr=tf)

    def ffn(xs, layer, idx, final_gain=None):
        return _ffn(xs, norm_gains[layer, 0 if idx == 0 else 2], w_in, w_out, final_gain,
                    layer=layer, idx=idx, tm=tm_ffn, tf=tf)

    xs = x.reshape(m, d)
    kv = None
    for layer in range(depth):
        xs = ffn(xs, layer, 0)
        if layer < n_a_layers:
            q_cols = a_w_qkv.shape[2] // 3
            qkv = _norm_matmul(xs, norm_gains[layer, 1], a_w_qkv[layer].astype(BF16), F32,
                               tm=tm_proj, tn=_tile(q_cols, 1536),
                               out_scale=SCORE_SCALE_LOG2, scaled_cols=q_cols, head_major=True)
            merged = _dilated_attn(qkv, slopes_a, batch=b, n_heads=heads_a)
            xs = _matmul_res(merged.reshape(m, -1), a_w_o[layer].astype(BF16), xs, tm=tm_proj)
        else:
            j = layer - n_a_layers
            q = _norm_matmul(xs, norm_gains[layer, 1], b_w_q[j].astype(BF16), BF16,
                             tm=tm_proj, tn=_tile(b_w_q.shape[2], 2048),
                             out_scale=SCORE_SCALE_LOG2, scaled_cols=b_w_q.shape[2])
            lambda_init = 0.8 - 0.6 * math.exp(-0.3 * layer)
            o = _diff_attn(q.reshape(b, s, -1), kv, slopes_b, b_lambda[j], b_subln_gain[j],
                           n_heads=heads_b, lambda_init=lambda_init, tq=_tile(s, 256))
            xs = _matmul_res(o.reshape(m, -1), b_w_o[j].astype(BF16), xs, tm=tm_proj)
        last = layer == depth - 1
        xs = ffn(xs, layer, 1, final_norm_gain if last else None)
        if layer == n_a_layers - 1:
            kv = _norm_matmul(xs, kv_norm_gain, b_w_kv.astype(BF16), BF16,
                              tm=tm_proj, tn=_tile(b_w_kv.shape[1], 2048)).reshape(b, s, -1)
    return xs.reshape(b, s, d)
```

```python
import functools
import math

import jax
import jax.numpy as jnp
from jax import lax
from jax.experimental import pallas as pl
from jax.experimental.pallas import tpu as pltpu

F32 = jnp.float32
BF16 = jnp.bfloat16

HEAD_DIM = 128
DILATION_PATTERNS = ((128, 1), (512, 4), (2048, 16))
N_BACK = 128
MACARON_WEIGHT = 0.5
RMS_EPS = 1e-6
SUBLN_EPS = 1e-5
LOG2_E = math.log2(math.e)
SCORE_SCALE_LOG2 = HEAD_DIM ** -0.5 * LOG2_E
SINGLE_OP_STRIDE = 4
MASK_VALUE = -1e30
V7X_VMEM_LIMIT_BYTES = 56 * 1024 * 1024


def _params(semantics):
    return pltpu.CompilerParams(dimension_semantics=semantics,
                                vmem_limit_bytes=V7X_VMEM_LIMIT_BYTES)


def _rms_normed(x, gain, eps):
    y = x * lax.rsqrt(jnp.mean(x * x, axis=-1, keepdims=True) + eps)
    return y * gain


def _dot(a, b):
    return jnp.dot(a, b, preferred_element_type=F32)


def _dot_nt(a, b):
    return lax.dot_general(a, b, (((1,), (1,)), ((), ())), preferred_element_type=F32)


def _pack_w_in_body(w_ref, o_ref, *, d_ff, d_ff_pad):
    zeros = jnp.zeros((o_ref.shape[0], d_ff_pad - d_ff), o_ref.dtype)
    for half in range(2):
        o_ref[:, half * d_ff_pad:half * d_ff_pad + d_ff] = (
            w_ref[:, half * d_ff:(half + 1) * d_ff].astype(o_ref.dtype))
        if d_ff_pad > d_ff:
            o_ref[:, half * d_ff_pad + d_ff:(half + 1) * d_ff_pad] = zeros


def _pack_w_in(w, d_ff_pad, *, tr):
    n, d, two_f = w.shape
    d_ff = two_f // 2
    return pl.pallas_call(
        functools.partial(_pack_w_in_body, d_ff=d_ff, d_ff_pad=d_ff_pad),
        grid=(n, d // tr),
        in_specs=[pl.BlockSpec((None, tr, two_f), lambda i, r: (i, r, 0))],
        out_specs=pl.BlockSpec((None, tr, 2 * d_ff_pad), lambda i, r: (i, r, 0)),
        out_shape=jax.ShapeDtypeStruct((n, d, 2 * d_ff_pad), BF16),
        compiler_params=_params(("parallel", "parallel")),
        name="pack_w_in",
    )(w)


def _pack_w_out_body(w_ref, o_ref, *, d_ff):
    tr = o_ref.shape[0]
    row = pl.program_id(1) * tr + lax.broadcasted_iota(jnp.int32, (tr, 1), 0)
    o_ref[...] = jnp.where(row < d_ff, w_ref[...], 0.0).astype(o_ref.dtype)


def _pack_w_out(w, d_ff_pad, *, tr):
    n, d_ff, d = w.shape
    return pl.pallas_call(
        functools.partial(_pack_w_out_body, d_ff=d_ff),
        grid=(n, d_ff_pad // tr),
        in_specs=[pl.BlockSpec((None, tr, d), lambda i, r: (i, r, 0))],
        out_specs=pl.BlockSpec((None, tr, d), lambda i, r: (i, r, 0)),
        out_shape=jax.ShapeDtypeStruct((n, d_ff_pad, d), BF16),
        compiler_params=_params(("parallel", "parallel")),
        name="pack_w_out",
    )(w)


def _ffn_body(x_ref, g_ref, wg_ref, wu_ref, wo_ref, fg_ref, o_ref, xn_ref, *, final_norm):
    f = pl.program_id(1)

    @pl.when(f == 0)
    def _():
        x = x_ref[...]
        xn_ref[...] = _rms_normed(x, g_ref[...], RMS_EPS).astype(BF16)
        o_ref[...] = x

    xn = xn_ref[...]
    gate = _dot(xn, wg_ref[...])
    up = _dot(xn, wu_ref[...])
    h = (gate * jax.nn.sigmoid(gate) * up * MACARON_WEIGHT).astype(BF16)
    o_ref[...] += _dot(h, wo_ref[...])

    if final_norm:
        @pl.when(f == pl.num_programs(1) - 1)
        def _():
            o_ref[...] = _rms_normed(o_ref[...], fg_ref[...], RMS_EPS)


def _ffn(x2d, gain, w_in, w_out, final_gain, *, layer, idx, tm, tf):
    m, d = x2d.shape
    fp = w_out.shape[1]
    n_f = fp // tf
    wi = 2 * layer + idx
    final_norm = final_gain is not None
    fg = final_gain if final_norm else gain
    return pl.pallas_call(
        functools.partial(_ffn_body, final_norm=final_norm),
        grid=(m // tm, n_f),
        in_specs=[
            pl.BlockSpec((tm, d), lambda i, f: (i, 0)),
            pl.BlockSpec((1, d), lambda i, f: (0, 0)),
            pl.BlockSpec((None, d, tf), lambda i, f: (wi, 0, f)),
            pl.BlockSpec((None, d, tf), lambda i, f: (wi, 0, n_f + f)),
            pl.BlockSpec((None, tf, d), lambda i, f: (wi, f, 0)),
            pl.BlockSpec((1, d), lambda i, f: (0, 0)),
        ],
        out_specs=pl.BlockSpec((tm, d), lambda i, f: (i, 0)),
        out_shape=jax.ShapeDtypeStruct((m, d), F32),
        scratch_shapes=[pltpu.VMEM((tm, d), BF16)],
        compiler_params=_params(("parallel", "arbitrary")),
        name="ffn",
    )(x2d, gain.reshape(1, d), w_in, w_in, w_out, fg.reshape(1, d))


def _norm_matmul_body(x_ref, g_ref, w_ref, o_ref, xn_ref, *, out_scale, scaled_tiles, head_major):
    j = pl.program_id(1)

    @pl.when(j == 0)
    def _():
        xn_ref[...] = _rms_normed(x_ref[...], g_ref[...], RMS_EPS).astype(BF16)

    acc = _dot(xn_ref[...], w_ref[...])
    if scaled_tiles:
        acc = acc * jnp.where(j < scaled_tiles, out_scale, 1.0)
    if head_major:
        for hh in range(o_ref.shape[0]):
            o_ref[hh] = acc[:, hh * HEAD_DIM:(hh + 1) * HEAD_DIM].astype(o_ref.dtype)
    else:
        o_ref[...] = acc.astype(o_ref.dtype)


def _norm_matmul(x2d, gain, w, out_dtype, *, tm, tn, out_scale=1.0, scaled_cols=0, head_major=False):
    m, d = x2d.shape
    n = w.shape[1]
    assert scaled_cols % tn == 0
    if head_major:
        out_spec = pl.BlockSpec((tn // HEAD_DIM, tm, HEAD_DIM), lambda i, j: (j, i, 0))
        out_shape = jax.ShapeDtypeStruct((n // HEAD_DIM, m, HEAD_DIM), out_dtype)
    else:
        out_spec = pl.BlockSpec((tm, tn), lambda i, j: (i, j))
        out_shape = jax.ShapeDtypeStruct((m, n), out_dtype)
    return pl.pallas_call(
        functools.partial(_norm_matmul_body, out_scale=out_scale, scaled_tiles=scaled_cols // tn,
                          head_major=head_major),
        grid=(m // tm, n // tn),
        in_specs=[
            pl.BlockSpec((tm, d), lambda i, j: (i, 0)),
            pl.BlockSpec((1, d), lambda i, j: (0, 0)),
            pl.BlockSpec((d, tn), lambda i, j: (0, j)),
        ],
        out_specs=out_spec,
        out_shape=out_shape,
        scratch_shapes=[pltpu.VMEM((tm, d), BF16)],
        compiler_params=_params(("parallel", "arbitrary")),
        name="norm_matmul",
    )(x2d, gain.reshape(1, d), w)


def _matmul_res_body(a_ref, w_ref, x_ref, o_ref):
    o_ref[...] = x_ref[...] + _dot(a_ref[...], w_ref[...])


def _matmul_res(a2d, w, x2d, *, tm):
    m, k = a2d.shape
    d = w.shape[1]
    return pl.pallas_call(
        _matmul_res_body,
        grid=(m // tm,),
        in_specs=[
            pl.BlockSpec((tm, k), lambda i: (i, 0)),
            pl.BlockSpec((k, d), lambda i: (0, 0), pipeline_mode=pl.Buffered(1)),
            pl.BlockSpec((tm, d), lambda i: (i, 0)),
        ],
        out_specs=pl.BlockSpec((tm, d), lambda i: (i, 0)),
        out_shape=jax.ShapeDtypeStruct((m, d), F32),
        compiler_params=_params(("parallel",)),
        name="matmul_res",
    )(a2d, w, x2d)


def _bmm(a, b):
    return lax.dot_general(a, b, (((2,), (1,)), ((0,), (0,))), preferred_element_type=F32)


def _bmm_nt(a, b):
    return lax.dot_general(a, b, (((2,), (2,)), ((0,), (0,))), preferred_element_type=F32)


def _residue_rows(c, sub_len, dilation):
    return pl.ds(c, sub_len) if dilation == 1 else pl.ds(c, sub_len, stride=dilation)


def _dilated_attn_body(slopes_ref, *refs, seq):
    n_groups = len(DILATION_PATTERNS)
    qkv_refs = refs[:3 * n_groups]
    o_ref, og_ref, lse_ref, stage_ref = refs[3 * n_groups:]
    slope = slopes_ref[pl.program_id(1)]

    qi = lax.broadcasted_iota(jnp.int32, (N_BACK, N_BACK), 0)
    ki = lax.broadcasted_iota(jnp.int32, (N_BACK, N_BACK), 1)
    steps_cur = (qi - ki).astype(F32)
    steps_prev = steps_cur + float(N_BACK)

    outs, lses = [], []
    for g, (window, dilation) in enumerate(DILATION_PATTERNS):
        q_ref, k_ref, v_ref = qkv_refs[3 * g:3 * g + 3]
        sub_len = seq // dilation
        n_blk = sub_len // N_BACK

        def blocks(ref, which):
            if dilation > SINGLE_OP_STRIDE:
                outer = dilation // SINGLE_OP_STRIDE
                for r in range(SINGLE_OP_STRIDE):
                    stage_ref[which, r] = ref[pl.ds(r, seq // SINGLE_OP_STRIDE, stride=SINGLE_OP_STRIDE), :]
                parts = [stage_ref[which, c % SINGLE_OP_STRIDE,
                                   pl.ds(c // SINGLE_OP_STRIDE, sub_len, stride=outer), :]
                         .reshape(n_blk, N_BACK, HEAD_DIM) for c in range(dilation)]
            else:
                parts = [ref[_residue_rows(c, sub_len, dilation), :].reshape(n_blk, N_BACK, HEAD_DIM)
                         for c in range(dilation)]
            return parts[0] if dilation == 1 else jnp.concatenate(parts, axis=0)

        def later(a):
            return jnp.concatenate([a[c * n_blk + 1:(c + 1) * n_blk] for c in range(dilation)], axis=0)

        def earlier(a):
            return jnp.concatenate([a[c * n_blk:(c + 1) * n_blk - 1] for c in range(dilation)], axis=0)

        def pad_first(a, fill):
            first = jnp.full((1,) + a.shape[1:], fill, a.dtype)
            parts = []
            for c in range(dilation):
                parts += [first, a[c * (n_blk - 1):(c + 1) * (n_blk - 1)]]
            return jnp.concatenate(parts, axis=0)

        q = blocks(q_ref, 0).astype(BF16)
        k = blocks(k_ref, 1).astype(BF16)
        v = blocks(v_ref, 2).astype(BF16)
        v_ones = jnp.concatenate([v, jnp.ones_like(v)], axis=-1)

        coef = -slope * (float(dilation) * LOG2_E)
        s_cur = _bmm_nt(q, k) + jnp.where(ki <= qi, coef * steps_cur, MASK_VALUE)
        m = jnp.max(s_cur, axis=-1, keepdims=True)
        if n_blk > 1:
            s_prev = _bmm_nt(later(q), earlier(k)) + jnp.where(ki >= qi, coef * steps_prev, MASK_VALUE)
            m = jnp.maximum(m, pad_first(jnp.max(s_prev, axis=-1, keepdims=True), MASK_VALUE))
            p_prev = jnp.exp2(s_prev - later(m)).astype(BF16)
        ol = _bmm(jnp.exp2(s_cur - m).astype(BF16), v_ones)
        if n_blk > 1:
            ol = ol + pad_first(_bmm(p_prev, earlier(v_ones)), 0.0)
        denom = ol[..., HEAD_DIM:]
        o = ol[..., :HEAD_DIM] / denom
        lse = m + jnp.log2(denom)
        if dilation == 1:
            outs.append(o.reshape(seq, HEAD_DIM))
            lses.append(lse.reshape(seq, HEAD_DIM))
        else:
            for which, (dst_ref, val) in enumerate(((og_ref, o), (lse_ref, lse))):
                if dilation > SINGLE_OP_STRIDE:
                    outer = dilation // SINGLE_OP_STRIDE
                    for c in range(dilation):
                        stage_ref[which, c % SINGLE_OP_STRIDE,
                                  pl.ds(c // SINGLE_OP_STRIDE, sub_len, stride=outer), :] = (
                            val[c * n_blk:(c + 1) * n_blk].reshape(sub_len, HEAD_DIM))
                    for r in range(SINGLE_OP_STRIDE):
                        dst_ref[g - 1, pl.ds(r, seq // SINGLE_OP_STRIDE, stride=SINGLE_OP_STRIDE), :] = (
                            stage_ref[which, r])
                else:
                    for c in range(dilation):
                        dst_ref[g - 1, _residue_rows(c, sub_len, dilation), :] = (
                            val[c * n_blk:(c + 1) * n_blk].reshape(sub_len, HEAD_DIM))
            outs.append(og_ref[g - 1])
            lses.append(lse_ref[g - 1])

    top = functools.reduce(jnp.maximum, lses)
    ws = [jnp.exp2(l - top) for l in lses]
    num = sum(w * o for w, o in zip(ws, outs))
    o_ref[...] = (num / sum(ws)).astype(o_ref.dtype)


def _dilated_attn(qkv, slopes, *, batch, n_heads):
    s = qkv.shape[1] // batch
    n_groups = len(DILATION_PATTERNS)

    def slab_spec(which, g):
        base = (which * n_groups + g) * n_heads
        return pl.BlockSpec((None, s, HEAD_DIM), lambda bi, h, base=base: (base + h, bi, 0))

    in_specs = [pl.BlockSpec(memory_space=pltpu.SMEM)]
    for g in range(n_groups):
        in_specs += [slab_spec(0, g), slab_spec(1, g), slab_spec(2, g)]
    return pl.pallas_call(
        functools.partial(_dilated_attn_body, seq=s),
        grid=(batch, n_heads),
        in_specs=in_specs,
        out_specs=pl.BlockSpec((None, s, HEAD_DIM), lambda bi, h: (bi, 0, h)),
        out_shape=jax.ShapeDtypeStruct((batch, s, n_heads * HEAD_DIM), BF16),
        scratch_shapes=[pltpu.VMEM((n_groups - 1, s, HEAD_DIM), F32),
                        pltpu.VMEM((n_groups - 1, s, HEAD_DIM), F32),
                        pltpu.VMEM((3, SINGLE_OP_STRIDE, s // SINGLE_OP_STRIDE, HEAD_DIM), F32)],
        compiler_params=_params(("parallel", "parallel")),
        name="dilated_attn",
    )(slopes, *([qkv] * (3 * n_groups)))


def _diff_attn_body(slopes_ref, lam_ref, q_ref, k_ref, v_ref, gain_ref, o_ref, *, seq, tq, lambda_init):
    slope = slopes_ref[pl.program_id(1)] * LOG2_E
    vdim = 2 * HEAD_DIM

    lam = lam_ref[...]
    lam_full = (jnp.exp(jnp.sum(lam[0:1] * lam[1:2], axis=-1, keepdims=True))
                - jnp.exp(jnp.sum(lam[2:3] * lam[3:4], axis=-1, keepdims=True))
                + lambda_init)

    key_bias = slope * lax.broadcasted_iota(jnp.int32, (1, seq), 1).astype(F32)
    qi = lax.broadcasted_iota(jnp.int32, (tq, tq), 0)
    ki = lax.broadcasted_iota(jnp.int32, (tq, tq), 1)
    causal = jnp.where(qi >= ki, 0.0, MASK_VALUE)

    for i in range(seq // tq):
        lo, hi = i * tq, (i + 1) * tq
        probs, denoms = [], []
        for cols in (slice(0, HEAD_DIM), slice(HEAD_DIM, vdim)):
            q = q_ref[lo:hi, cols]
            tiles = []
            m = None
            for j in range(i + 1):
                klo, khi = j * tq, (j + 1) * tq
                bias = key_bias[:, klo:khi] + causal if j == i else key_bias[:, klo:khi]
                s = _dot_nt(q, k_ref[klo:khi, cols]) + bias
                tile_max = jnp.max(s, axis=-1, keepdims=True)
                m = tile_max if m is None else jnp.maximum(m, tile_max)
                tiles.append(s)
            l = None
            ps = []
            for s in tiles:
                p = jnp.exp2(s - m)
                tile_sum = jnp.sum(p, axis=-1, keepdims=True)
                l = tile_sum if l is None else l + tile_sum
                ps.append(p.astype(BF16))
            probs.append(ps[0] if len(ps) == 1 else jnp.concatenate(ps, axis=1))
            denoms.append(l)
        pv = _dot(jnp.concatenate(probs, axis=0), v_ref[0:hi, :])
        o = pv[:tq] / denoms[0] - lam_full * (pv[tq:] / denoms[1])
        o = _rms_normed(o, gain_ref[...], SUBLN_EPS) * (1.0 - lambda_init)
        o_ref[lo:hi, :] = o.astype(o_ref.dtype)


def _diff_attn(q, kv, slopes, lam, subln_gain, *, n_heads, lambda_init, tq):
    b, s, _ = q.shape
    vdim = 2 * HEAD_DIM
    return pl.pallas_call(
        functools.partial(_diff_attn_body, seq=s, tq=tq, lambda_init=lambda_init),
        grid=(b, n_heads),
        in_specs=[
            pl.BlockSpec(memory_space=pltpu.SMEM),
            pl.BlockSpec((4, HEAD_DIM), lambda bi, h: (0, 0)),
            pl.BlockSpec((None, s, vdim), lambda bi, h: (bi, 0, h)),
            pl.BlockSpec((None, s, vdim), lambda bi, h: (bi, 0, h)),
            pl.BlockSpec((None, s, vdim), lambda bi, h: (bi, 0, n_heads + h)),
            pl.BlockSpec((1, vdim), lambda bi, h: (0, 0)),
        ],
        out_specs=pl.BlockSpec((None, s, vdim), lambda bi, h: (bi, 0, h)),
        out_shape=jax.ShapeDtypeStruct((b, s, n_heads * vdim), BF16),
        compiler_params=_params(("parallel", "parallel")),
        name="diff_attn",
    )(slopes, lam, q, kv, kv, subln_gain.reshape(1, vdim))


def _alibi_slopes(n_heads):
    return 2.0 ** (-8.0 * jnp.arange(1, n_heads + 1, dtype=F32) / n_heads)


def _tile(total, preferred):
    return preferred if total % preferred == 0 else total


def kernel(x, norm_gains, ffn_w_in, ffn_w_out, a_w_qkv, a_w_o, kv_norm_gain, b_w_kv, b_w_q, b_lambda,
           b_subln_gain, b_w_o, final_norm_gain):
    b, s, d = x.shape
    m = b * s
    depth = norm_gains.shape[0]
    n_a_layers = a_w_qkv.shape[0]
    d_ff = ffn_w_out.shape[2]
    heads_a = a_w_o.shape[1] // HEAD_DIM
    heads_b = b_w_o.shape[1] // (2 * HEAD_DIM)

    tm_proj = _tile(m, 1024)
    tm_ffn = _tile(m, 1024)
    tf = 512
    d_ff_pad = -(-d_ff // tf) * tf

    slopes_a = _alibi_slopes(heads_a)
    slopes_b = _alibi_slopes(heads_b)

    w_in = _pack_w_in(ffn_w_in.reshape(2 * depth, d, 2 * d_ff), d_ff_pad, tr=_tile(d, 128))
    w_out = _pack_w_out(ffn_w_out.reshape(2 * depth, d_ff, d), d_ff_pad, tr=tf)

    def ffn(xs, layer, idx, final_gain=None):
        return _ffn(xs, norm_gains[layer, 0 if idx == 0 else 2], w_in, w_out, final_gain,
                    layer=layer, idx=idx, tm=tm_ffn, tf=tf)

    xs = x.reshape(m, d)
    kv = None
    for layer in range(depth):
        xs = ffn(xs, layer, 0)
        if layer < n_a_layers:
            q_cols = a_w_qkv.shape[2] // 3
            qkv = _norm_matmul(xs, norm_gains[layer, 1], a_w_qkv[layer].astype(BF16), F32,
                               tm=tm_proj, tn=_tile(q_cols, 1536),
                               out_scale=SCORE_SCALE_LOG2, scaled_cols=q_cols, head_major=True)
            merged = _dilated_attn(qkv, slopes_a, batch=b, n_heads=heads_a)
            xs = _matmul_res(merged.reshape(m, -1), a_w_o[layer].astype(BF16), xs, tm=tm_proj)
        else:
            j = layer - n_a_layers
            q = _norm_matmul(xs, norm_gains[layer, 1], b_w_q[j].astype(BF16), BF16,
                             tm=tm_proj, tn=_tile(b_w_q.shape[2], 2048),
                             out_scale=SCORE_SCALE_LOG2, scaled_cols=b_w_q.shape[2])
            lambda_init = 0.8 - 0.6 * math.exp(-0.3 * layer)
            o = _diff_attn(q.reshape(b, s, -1), kv, slopes_b, b_lambda[j], b_subln_gain[j],
                           n_heads=heads_b, lambda_init=lambda_init, tq=_tile(s, 256))
            xs = _matmul_res(o.reshape(m, -1), b_w_o[j].astype(BF16), xs, tm=tm_proj)
        last = layer == depth - 1
        xs = ffn(xs, layer, 1, final_norm_gain if last else None)
        if layer == n_a_layers - 1:
            kv = _norm_matmul(xs, kv_norm_gain, b_w_kv.astype(BF16), BF16,
                              tm=tm_proj, tn=_tile(b_w_kv.shape[1], 2048)).reshape(b, s, -1)
    return xs.reshape(b, s, d)
```

```python
import functools
import math

import jax
import jax.numpy as jnp
from jax import lax
from jax.experimental import pallas as pl
from jax.experimental.pallas import tpu as pltpu

F32 = jnp.float32
BF16 = jnp.bfloat16

HEAD_DIM = 128
DILATION_PATTERNS = ((128, 1), (512, 4), (2048, 16))
N_BACK = 128
MACARON_WEIGHT = 0.5
RMS_EPS = 1e-6
SUBLN_EPS = 1e-5
LOG2_E = math.log2(math.e)
SCORE_SCALE_LOG2 = HEAD_DIM ** -0.5 * LOG2_E
MAX_HEADS_PER_STEP = 2
SINGLE_OP_STRIDE = 4
MASK_VALUE = -1e30
V7X_VMEM_LIMIT_BYTES = 56 * 1024 * 1024


def _params(semantics):
    return pltpu.CompilerParams(dimension_semantics=semantics,
                                vmem_limit_bytes=V7X_VMEM_LIMIT_BYTES)


def _rms_normed(x, gain, eps):
    y = x * lax.rsqrt(jnp.mean(x * x, axis=-1, keepdims=True) + eps)
    return y * gain


def _dot(a, b):
    return jnp.dot(a, b, preferred_element_type=F32)


def _dot_nt(a, b):
    return lax.dot_general(a, b, (((1,), (1,)), ((), ())), preferred_element_type=F32)


def _pack_w_in_body(w_ref, o_ref, *, d_ff, d_ff_pad):
    zeros = jnp.zeros((o_ref.shape[0], d_ff_pad - d_ff), o_ref.dtype)
    for half in range(2):
        o_ref[:, half * d_ff_pad:half * d_ff_pad + d_ff] = (
            w_ref[:, half * d_ff:(half + 1) * d_ff].astype(o_ref.dtype))
        if d_ff_pad > d_ff:
            o_ref[:, half * d_ff_pad + d_ff:(half + 1) * d_ff_pad] = zeros


def _pack_w_in(w, d_ff_pad, *, tr):
    n, d, two_f = w.shape
    d_ff = two_f // 2
    return pl.pallas_call(
        functools.partial(_pack_w_in_body, d_ff=d_ff, d_ff_pad=d_ff_pad),
        grid=(n, d // tr),
        in_specs=[pl.BlockSpec((None, tr, two_f), lambda i, r: (i, r, 0))],
        out_specs=pl.BlockSpec((None, tr, 2 * d_ff_pad), lambda i, r: (i, r, 0)),
        out_shape=jax.ShapeDtypeStruct((n, d, 2 * d_ff_pad), BF16),
        compiler_params=_params(("parallel", "parallel")),
        name="pack_w_in",
    )(w)


def _pack_w_out_body(w_ref, o_ref, *, d_ff):
    tr = o_ref.shape[0]
    row = pl.program_id(1) * tr + lax.broadcasted_iota(jnp.int32, (tr, 1), 0)
    o_ref[...] = jnp.where(row < d_ff, w_ref[...], 0.0).astype(o_ref.dtype)


def _pack_w_out(w, d_ff_pad, *, tr):
    n, d_ff, d = w.shape
    return pl.pallas_call(
        functools.partial(_pack_w_out_body, d_ff=d_ff),
        grid=(n, d_ff_pad // tr),
        in_specs=[pl.BlockSpec((None, tr, d), lambda i, r: (i, r, 0))],
        out_specs=pl.BlockSpec((None, tr, d), lambda i, r: (i, r, 0)),
        out_shape=jax.ShapeDtypeStruct((n, d_ff_pad, d), BF16),
        compiler_params=_params(("parallel", "parallel")),
        name="pack_w_out",
    )(w)


def _ffn_body(x_ref, g_ref, wg_ref, wu_ref, wo_ref, fg_ref, o_ref, xn_ref, *, final_norm):
    f = pl.program_id(1)

    @pl.when(f == 0)
    def _():
        x = x_ref[...]
        xn_ref[...] = _rms_normed(x, g_ref[...], RMS_EPS).astype(BF16)
        o_ref[...] = x

    xn = xn_ref[...]
    gate = _dot(xn, wg_ref[...])
    up = _dot(xn, wu_ref[...])
    h = (gate * jax.nn.sigmoid(gate) * up * MACARON_WEIGHT).astype(BF16)
    o_ref[...] += _dot(h, wo_ref[...])

    if final_norm:
        @pl.when(f == pl.num_programs(1) - 1)
        def _():
            o_ref[...] = _rms_normed(o_ref[...], fg_ref[...], RMS_EPS)


def _ffn(x2d, gain, w_in, w_out, final_gain, *, layer, idx, tm, tf):
    m, d = x2d.shape
    fp = w_out.shape[1]
    n_f = fp // tf
    wi = 2 * layer + idx
    final_norm = final_gain is not None
    fg = final_gain if final_norm else gain
    return pl.pallas_call(
        functools.partial(_ffn_body, final_norm=final_norm),
        grid=(m // tm, n_f),
        in_specs=[
            pl.BlockSpec((tm, d), lambda i, f: (i, 0)),
            pl.BlockSpec((1, d), lambda i, f: (0, 0)),
            pl.BlockSpec((None, d, tf), lambda i, f: (wi, 0, f)),
            pl.BlockSpec((None, d, tf), lambda i, f: (wi, 0, n_f + f)),
            pl.BlockSpec((None, tf, d), lambda i, f: (wi, f, 0)),
            pl.BlockSpec((1, d), lambda i, f: (0, 0)),
        ],
        out_specs=pl.BlockSpec((tm, d), lambda i, f: (i, 0)),
        out_shape=jax.ShapeDtypeStruct((m, d), F32),
        scratch_shapes=[pltpu.VMEM((tm, d), BF16)],
        compiler_params=_params(("parallel", "arbitrary")),
        name="ffn",
    )(x2d, gain.reshape(1, d), w_in, w_in, w_out, fg.reshape(1, d))


def _norm_matmul_body(x_ref, g_ref, w_ref, o_ref, xn_ref, *, out_scale, scaled_tiles, head_major):
    j = pl.program_id(1)

    @pl.when(j == 0)
    def _():
        xn_ref[...] = _rms_normed(x_ref[...], g_ref[...], RMS_EPS).astype(BF16)

    acc = _dot(xn_ref[...], w_ref[...])
    if scaled_tiles:
        acc = acc * jnp.where(j < scaled_tiles, out_scale, 1.0)
    if head_major:
        for hh in range(o_ref.shape[0]):
            o_ref[hh] = acc[:, hh * HEAD_DIM:(hh + 1) * HEAD_DIM].astype(o_ref.dtype)
    else:
        o_ref[...] = acc.astype(o_ref.dtype)


def _norm_matmul(x2d, gain, w, out_dtype, *, tm, tn, out_scale=1.0, scaled_cols=0, head_major=False):
    m, d = x2d.shape
    n = w.shape[1]
    assert scaled_cols % tn == 0
    if head_major:
        out_spec = pl.BlockSpec((tn // HEAD_DIM, tm, HEAD_DIM), lambda i, j: (j, i, 0))
        out_shape = jax.ShapeDtypeStruct((n // HEAD_DIM, m, HEAD_DIM), out_dtype)
    else:
        out_spec = pl.BlockSpec((tm, tn), lambda i, j: (i, j))
        out_shape = jax.ShapeDtypeStruct((m, n), out_dtype)
    return pl.pallas_call(
        functools.partial(_norm_matmul_body, out_scale=out_scale, scaled_tiles=scaled_cols // tn,
                          head_major=head_major),
        grid=(m // tm, n // tn),
        in_specs=[
            pl.BlockSpec((tm, d), lambda i, j: (i, 0)),
            pl.BlockSpec((1, d), lambda i, j: (0, 0)),
            pl.BlockSpec((d, tn), lambda i, j: (0, j)),
        ],
        out_specs=out_spec,
        out_shape=out_shape,
        scratch_shapes=[pltpu.VMEM((tm, d), BF16)],
        compiler_params=_params(("parallel", "arbitrary")),
        name="norm_matmul",
    )(x2d, gain.reshape(1, d), w)


def _matmul_res_body(a_ref, w_ref, x_ref, o_ref):
    o_ref[...] = x_ref[...] + _dot(a_ref[...], w_ref[...])


def _matmul_res(a2d, w, x2d, *, tm):
    m, k = a2d.shape
    d = w.shape[1]
    return pl.pallas_call(
        _matmul_res_body,
        grid=(m // tm,),
        in_specs=[
            pl.BlockSpec((tm, k), lambda i: (i, 0)),
            pl.BlockSpec((k, d), lambda i: (0, 0), pipeline_mode=pl.Buffered(1)),
            pl.BlockSpec((tm, d), lambda i: (i, 0)),
        ],
        out_specs=pl.BlockSpec((tm, d), lambda i: (i, 0)),
        out_shape=jax.ShapeDtypeStruct((m, d), F32),
        compiler_params=_params(("parallel",)),
        name="matmul_res",
    )(a2d, w, x2d)


def _bmm(a, b):
    return lax.dot_general(a, b, (((2,), (1,)), ((0,), (0,))), preferred_element_type=F32)


def _bmm_nt(a, b):
    return lax.dot_general(a, b, (((2,), (2,)), ((0,), (0,))), preferred_element_type=F32)


def _residue_rows(c, sub_len, dilation):
    return pl.ds(c, sub_len) if dilation == 1 else pl.ds(c, sub_len, stride=dilation)


def _dilated_attn_body(slopes_ref, qkv_ref, o_ref, og_ref, lse_ref, stage_ref, *, seq, heads_per_step):
    n_groups = len(DILATION_PATTERNS)

    qi = lax.broadcasted_iota(jnp.int32, (N_BACK, N_BACK), 0)
    ki = lax.broadcasted_iota(jnp.int32, (N_BACK, N_BACK), 1)
    steps_cur = (qi - ki).astype(F32)
    steps_prev = steps_cur + float(N_BACK)

    for hh in range(heads_per_step):
        slope = slopes_ref[pl.program_id(1) * heads_per_step + hh]
        outs, lses = [], []
        for g, (window, dilation) in enumerate(DILATION_PATTERNS):
            q_ref, k_ref, v_ref = (qkv_ref.at[which, g, hh] for which in range(3))
            sub_len = seq // dilation
            n_blk = sub_len // N_BACK

            def blocks(ref, which):
                if dilation > SINGLE_OP_STRIDE:
                    outer = dilation // SINGLE_OP_STRIDE
                    for r in range(SINGLE_OP_STRIDE):
                        stage_ref[which, r] = ref[pl.ds(r, seq // SINGLE_OP_STRIDE, stride=SINGLE_OP_STRIDE), :]
                    parts = [stage_ref[which, c % SINGLE_OP_STRIDE,
                                       pl.ds(c // SINGLE_OP_STRIDE, sub_len, stride=outer), :]
                             .reshape(n_blk, N_BACK, HEAD_DIM) for c in range(dilation)]
                else:
                    parts = [ref[_residue_rows(c, sub_len, dilation), :].reshape(n_blk, N_BACK, HEAD_DIM)
                             for c in range(dilation)]
                return parts[0] if dilation == 1 else jnp.concatenate(parts, axis=0)

            def later(a):
                return jnp.concatenate([a[c * n_blk + 1:(c + 1) * n_blk] for c in range(dilation)], axis=0)

            def earlier(a):
                return jnp.concatenate([a[c * n_blk:(c + 1) * n_blk - 1] for c in range(dilation)], axis=0)

            def pad_first(a, fill):
                first = jnp.full((1,) + a.shape[1:], fill, a.dtype)
                parts = []
                for c in range(dilation):
                    parts += [first, a[c * (n_blk - 1):(c + 1) * (n_blk - 1)]]
                return jnp.concatenate(parts, axis=0)

            q = blocks(q_ref, 0).astype(BF16)
            k = blocks(k_ref, 1).astype(BF16)
            v = blocks(v_ref, 2).astype(BF16)
            v_ones = jnp.concatenate([v, jnp.ones_like(v)], axis=-1)

            coef = -slope * (float(dilation) * LOG2_E)
            s_cur = _bmm_nt(q, k) + jnp.where(ki <= qi, coef * steps_cur, MASK_VALUE)
            m = jnp.max(s_cur, axis=-1, keepdims=True)
            if n_blk > 1:
                s_prev = _bmm_nt(later(q), earlier(k)) + jnp.where(ki >= qi, coef * steps_prev, MASK_VALUE)
                m = jnp.maximum(m, pad_first(jnp.max(s_prev, axis=-1, keepdims=True), MASK_VALUE))
                p_prev = jnp.exp2(s_prev - later(m)).astype(BF16)
            ol = _bmm(jnp.exp2(s_cur - m).astype(BF16), v_ones)
            if n_blk > 1:
                ol = ol + pad_first(_bmm(p_prev, earlier(v_ones)), 0.0)
            denom = ol[..., HEAD_DIM:]
            o = ol[..., :HEAD_DIM] / denom
            lse = m + jnp.log2(denom)
            if dilation == 1:
                outs.append(o.reshape(seq, HEAD_DIM))
                lses.append(lse.reshape(seq, HEAD_DIM))
            else:
                for which, (dst_ref, val) in enumerate(((og_ref, o), (lse_ref, lse))):
                    if dilation > SINGLE_OP_STRIDE:
                        outer = dilation // SINGLE_OP_STRIDE
                        for c in range(dilation):
                            stage_ref[which, c % SINGLE_OP_STRIDE,
                                      pl.ds(c // SINGLE_OP_STRIDE, sub_len, stride=outer), :] = (
                                val[c * n_blk:(c + 1) * n_blk].reshape(sub_len, HEAD_DIM))
                        for r in range(SINGLE_OP_STRIDE):
                            dst_ref[g - 1, pl.ds(r, seq // SINGLE_OP_STRIDE, stride=SINGLE_OP_STRIDE), :] = (
                                stage_ref[which, r])
                    else:
                        for c in range(dilation):
                            dst_ref[g - 1, _residue_rows(c, sub_len, dilation), :] = (
                                val[c * n_blk:(c + 1) * n_blk].reshape(sub_len, HEAD_DIM))
                outs.append(og_ref[g - 1])
                lses.append(lse_ref[g - 1])

        top = functools.reduce(jnp.maximum, lses)
        ws = [jnp.exp2(l - top) for l in lses]
        num = sum(w * o for w, o in zip(ws, outs))
        o_ref[:, hh * HEAD_DIM:(hh + 1) * HEAD_DIM] = (num / sum(ws)).astype(o_ref.dtype)


def _dilated_attn(qkv, slopes, *, batch, n_heads):
    s = qkv.shape[1] // batch
    n_groups = len(DILATION_PATTERNS)
    hps = math.gcd(n_heads, MAX_HEADS_PER_STEP)

    qkv = qkv.reshape(3, n_groups, n_heads, batch * s, HEAD_DIM)
    in_specs = [pl.BlockSpec(memory_space=pltpu.SMEM),
                pl.BlockSpec((3, n_groups, hps, s, HEAD_DIM), lambda bi, h: (0, 0, h, bi, 0))]
    return pl.pallas_call(
        functools.partial(_dilated_attn_body, seq=s, heads_per_step=hps),
        grid=(batch, n_heads // hps),
        in_specs=in_specs,
        out_specs=pl.BlockSpec((None, s, hps * HEAD_DIM), lambda bi, h: (bi, 0, h)),
        out_shape=jax.ShapeDtypeStruct((batch, s, n_heads * HEAD_DIM), BF16),
        scratch_shapes=[pltpu.VMEM((n_groups - 1, s, HEAD_DIM), F32),
                        pltpu.VMEM((n_groups - 1, s, HEAD_DIM), F32),
                        pltpu.VMEM((3, SINGLE_OP_STRIDE, s // SINGLE_OP_STRIDE, HEAD_DIM), F32)],
        compiler_params=_params(("parallel", "parallel")),
        name="dilated_attn",
    )(slopes, qkv)


def _diff_attn_body(slopes_ref, lam_ref, q_ref, k_ref, v_ref, gain_ref, o_ref,
                    *, seq, tq, lambda_init, heads_per_step):
    vdim = 2 * HEAD_DIM

    lam = lam_ref[...]
    lam_full = (jnp.exp(jnp.sum(lam[0:1] * lam[1:2], axis=-1, keepdims=True))
                - jnp.exp(jnp.sum(lam[2:3] * lam[3:4], axis=-1, keepdims=True))
                + lambda_init)

    key_pos = lax.broadcasted_iota(jnp.int32, (1, seq), 1).astype(F32)
    qi = lax.broadcasted_iota(jnp.int32, (tq, tq), 0)
    ki = lax.broadcasted_iota(jnp.int32, (tq, tq), 1)
    causal = jnp.where(qi >= ki, 0.0, MASK_VALUE)

    for hh in range(heads_per_step):
        slope = slopes_ref[pl.program_id(1) * heads_per_step + hh] * LOG2_E
        c0 = hh * vdim
        key_bias = slope * key_pos

        for i in range(seq // tq):
            lo, hi = i * tq, (i + 1) * tq
            probs, denoms = [], []
            for cols in (slice(c0, c0 + HEAD_DIM), slice(c0 + HEAD_DIM, c0 + vdim)):
                q = q_ref[lo:hi, cols]
                tiles = []
                m = None
                for j in range(i + 1):
                    klo, khi = j * tq, (j + 1) * tq
                    bias = key_bias[:, klo:khi] + causal if j == i else key_bias[:, klo:khi]
                    s = _dot_nt(q, k_ref[klo:khi, cols]) + bias
                    tile_max = jnp.max(s, axis=-1, keepdims=True)
                    m = tile_max if m is None else jnp.maximum(m, tile_max)
                    tiles.append(s)
                l = None
                ps = []
                for s in tiles:
                    p = jnp.exp2(s - m)
                    tile_sum = jnp.sum(p, axis=-1, keepdims=True)
                    l = tile_sum if l is None else l + tile_sum
                    ps.append(p.astype(BF16))
                probs.append(ps[0] if len(ps) == 1 else jnp.concatenate(ps, axis=1))
                denoms.append(l)
            pv = _dot(jnp.concatenate(probs, axis=0), v_ref[0:hi, c0:c0 + vdim])
            o = pv[:tq] / denoms[0] - lam_full * (pv[tq:] / denoms[1])
            o = _rms_normed(o, gain_ref[...], SUBLN_EPS) * (1.0 - lambda_init)
            o_ref[lo:hi, c0:c0 + vdim] = o.astype(o_ref.dtype)


def _diff_attn(q, kv, slopes, lam, subln_gain, *, n_heads, lambda_init, tq):
    b, s, _ = q.shape
    vdim = 2 * HEAD_DIM
    hps = math.gcd(n_heads, MAX_HEADS_PER_STEP)
    return pl.pallas_call(
        functools.partial(_diff_attn_body, seq=s, tq=tq, lambda_init=lambda_init,
                          heads_per_step=hps),
        grid=(b, n_heads // hps),
        in_specs=[
            pl.BlockSpec(memory_space=pltpu.SMEM),
            pl.BlockSpec((4, HEAD_DIM), lambda bi, h: (0, 0)),
            pl.BlockSpec((None, s, hps * vdim), lambda bi, h: (bi, 0, h)),
            pl.BlockSpec((None, s, hps * vdim), lambda bi, h: (bi, 0, h)),
            pl.BlockSpec((None, s, hps * vdim), lambda bi, h: (bi, 0, n_heads // hps + h)),
            pl.BlockSpec((1, vdim), lambda bi, h: (0, 0)),
        ],
        out_specs=pl.BlockSpec((None, s, hps * vdim), lambda bi, h: (bi, 0, h)),
        out_shape=jax.ShapeDtypeStruct((b, s, n_heads * vdim), BF16),
        compiler_params=_params(("parallel", "parallel")),
        name="diff_attn",
    )(slopes, lam, q, kv, kv, subln_gain.reshape(1, vdim))


def _alibi_slopes(n_heads):
    return 2.0 ** (-8.0 * jnp.arange(1, n_heads + 1, dtype=F32) / n_heads)


def _tile(total, preferred):
    return preferred if total % preferred == 0 else total


def kernel(x, norm_gains, ffn_w_in, ffn_w_out, a_w_qkv, a_w_o, kv_norm_gain, b_w_kv, b_w_q, b_lambda,
           b_subln_gain, b_w_o, final_norm_gain):
    b, s, d = x.shape
    m = b * s
    depth = norm_gains.shape[0]
    n_a_layers = a_w_qkv.shape[0]
    d_ff = ffn_w_out.shape[2]
    heads_a = a_w_o.shape[1] // HEAD_DIM
    heads_b = b_w_o.shape[1] // (2 * HEAD_DIM)

    tm_proj = _tile(m, 1024)
    tm_ffn = _tile(m, 1024)
    tf = 512
    d_ff_pad = -(-d_ff // tf) * tf

    slopes_a = _alibi_slopes(heads_a)
    slopes_b = _alibi_slopes(heads_b)

    w_in = _pack_w_in(ffn_w_in.reshape(2 * depth, d, 2 * d_ff), d_ff_pad, tr=_tile(d, 128))
    w_out = _pack_w_out(ffn_w_out.reshape(2 * depth, d_ff, d), d_ff_pad, tr=tf)

    def ffn(xs, layer, idx, final_gain=None):
        return _ffn(xs, norm_gains[layer, 0 if idx == 0 else 2], w_in, w_out, final_gain,
                    layer=layer, idx=idx, tm=tm_ffn, tf=tf)

    xs = x.reshape(m, d)
    kv = None
    for layer in range(depth):
        xs = ffn(xs, layer, 0)
        if layer < n_a_layers:
            q_cols = a_w_qkv.shape[2] // 3
            qkv = _norm_matmul(xs, norm_gains[layer, 1], a_w_qkv[layer].astype(BF16), F32,
                               tm=tm_proj, tn=_tile(q_cols, 1536),
                               out_scale=SCORE_SCALE_LOG2, scaled_cols=q_cols, head_major=True)
            merged = _dilated_attn(qkv, slopes_a, batch=b, n_heads=heads_a)
            xs = _matmul_res(merged.reshape(m, -1), a_w_o[layer].astype(BF16), xs, tm=tm_proj)
        else:
            j = layer - n_a_layers
            q = _norm_matmul(xs, norm_gains[layer, 1], b_w_q[j].astype(BF16), BF16,
                             tm=tm_proj, tn=_tile(b_w_q.shape[2], 2048),
                             out_scale=SCORE_SCALE_LOG2, scaled_cols=b_w_q.shape[2])
            lambda_init = 0.8 - 0.6 * math.exp(-0.3 * layer)
            o = _diff_attn(q.reshape(b, s, -1), kv, slopes_b, b_lambda[j], b_subln_gain[j],
                           n_heads=heads_b, lambda_init=lambda_init, tq=_tile(s, 256))
            xs = _matmul_res(o.reshape(m, -1), b_w_o[j].astype(BF16), xs, tm=tm_proj)
        last = layer == depth - 1
        xs = ffn(xs, layer, 1, final_norm_gain if last else None)
        if layer == n_a_layers - 1:
            kv = _norm_matmul(xs, kv_norm_gain, b_w_kv.astype(BF16), BF16,
                              tm=tm_proj, tn=_tile(b_w_kv.shape[1], 2048)).reshape(b, s, -1)
    return xs.reshape(b, s, d)
```

```python
import functools
import math

import jax
import jax.numpy as jnp
from jax import lax
from jax.experimental import pallas as pl
from jax.experimental.pallas import tpu as pltpu

F32 = jnp.float32
BF16 = jnp.bfloat16

HEAD_DIM = 128
DILATION_PATTERNS = ((128, 1), (512, 4), (2048, 16))
N_BACK = 128
MACARON_WEIGHT = 0.5
RMS_EPS = 1e-6
SUBLN_EPS = 1e-5
LOG2_E = math.log2(math.e)
SCORE_SCALE_LOG2 = HEAD_DIM ** -0.5 * LOG2_E
MAX_HEADS_PER_STEP = 2
SINGLE_OP_STRIDE = 4
MASK_VALUE = -1e30
V7X_VMEM_LIMIT_BYTES = 56 * 1024 * 1024


def _params(semantics, fusible_inputs=None):
    return pltpu.CompilerParams(dimension_semantics=semantics,
                                vmem_limit_bytes=V7X_VMEM_LIMIT_BYTES,
                                allow_input_fusion=fusible_inputs)


def _rms_normed(x, gain, eps):
    y = x * lax.rsqrt(jnp.mean(x * x, axis=-1, keepdims=True) + eps)
    return y * gain


def _dot(a, b):
    return jnp.dot(a, b, preferred_element_type=F32)


def _dot_nt(a, b):
    return lax.dot_general(a, b, (((1,), (1,)), ((), ())), preferred_element_type=F32)


def _pack_w_in_body(w_ref, o_ref, *, d_ff, d_ff_pad):
    zeros = jnp.zeros((o_ref.shape[0], d_ff_pad - d_ff), o_ref.dtype)
    for half in range(2):
        o_ref[:, half * d_ff_pad:half * d_ff_pad + d_ff] = (
            w_ref[:, half * d_ff:(half + 1) * d_ff].astype(o_ref.dtype))
        if d_ff_pad > d_ff:
            o_ref[:, half * d_ff_pad + d_ff:(half + 1) * d_ff_pad] = zeros


def _pack_w_in(w, d_ff_pad, *, tr):
    n, d, two_f = w.shape
    d_ff = two_f // 2
    return pl.pallas_call(
        functools.partial(_pack_w_in_body, d_ff=d_ff, d_ff_pad=d_ff_pad),
        grid=(n, d // tr),
        in_specs=[pl.BlockSpec((None, tr, two_f), lambda i, r: (i, r, 0))],
        out_specs=pl.BlockSpec((None, tr, 2 * d_ff_pad), lambda i, r: (i, r, 0)),
        out_shape=jax.ShapeDtypeStruct((n, d, 2 * d_ff_pad), BF16),
        compiler_params=_params(("parallel", "parallel")),
        name="pack_w_in",
    )(w)


def _pack_w_out_body(w_ref, o_ref, *, d_ff):
    tr = o_ref.shape[0]
    row = pl.program_id(1) * tr + lax.broadcasted_iota(jnp.int32, (tr, 1), 0)
    o_ref[...] = jnp.where(row < d_ff, w_ref[...], 0.0).astype(o_ref.dtype)


def _pack_w_out(w, d_ff_pad, *, tr):
    n, d_ff, d = w.shape
    return pl.pallas_call(
        functools.partial(_pack_w_out_body, d_ff=d_ff),
        grid=(n, d_ff_pad // tr),
        in_specs=[pl.BlockSpec((None, tr, d), lambda i, r: (i, r, 0))],
        out_specs=pl.BlockSpec((None, tr, d), lambda i, r: (i, r, 0)),
        out_shape=jax.ShapeDtypeStruct((n, d_ff_pad, d), BF16),
        compiler_params=_params(("parallel", "parallel")),
        name="pack_w_out",
    )(w)


def _ffn_body(x_ref, g_ref, wg_ref, wu_ref, wo_ref, fg_ref, o_ref, xn_ref, *, final_norm):
    f = pl.program_id(1)

    @pl.when(f == 0)
    def _():
        x = x_ref[...]
        xn_ref[...] = _rms_normed(x, g_ref[...], RMS_EPS).astype(BF16)
        o_ref[...] = x

    xn = xn_ref[...]
    gate = _dot(xn, wg_ref[...])
    up = _dot(xn, wu_ref[...])
    h = (gate * jax.nn.sigmoid(gate) * up * MACARON_WEIGHT).astype(BF16)
    o_ref[...] += _dot(h, wo_ref[...])

    if final_norm:
        @pl.when(f == pl.num_programs(1) - 1)
        def _():
            o_ref[...] = _rms_normed(o_ref[...], fg_ref[...], RMS_EPS)


def _ffn(x2d, gain, w_in, w_out, final_gain, *, layer, idx, tm, tf):
    m, d = x2d.shape
    fp = w_out.shape[1]
    n_f = fp // tf
    wi = 2 * layer + idx
    final_norm = final_gain is not None
    fg = final_gain if final_norm else gain
    return pl.pallas_call(
        functools.partial(_ffn_body, final_norm=final_norm),
        grid=(m // tm, n_f),
        in_specs=[
            pl.BlockSpec((tm, d), lambda i, f: (i, 0)),
            pl.BlockSpec((1, d), lambda i, f: (0, 0)),
            pl.BlockSpec((None, d, tf), lambda i, f: (wi, 0, f)),
            pl.BlockSpec((None, d, tf), lambda i, f: (wi, 0, n_f + f)),
            pl.BlockSpec((None, tf, d), lambda i, f: (wi, f, 0)),
            pl.BlockSpec((1, d), lambda i, f: (0, 0)),
        ],
        out_specs=pl.BlockSpec((tm, d), lambda i, f: (i, 0)),
        out_shape=jax.ShapeDtypeStruct((m, d), F32),
        scratch_shapes=[pltpu.VMEM((tm, d), BF16)],
        compiler_params=_params(("parallel", "arbitrary")),
        name="ffn",
    )(x2d, gain.reshape(1, d), w_in, w_in, w_out, fg.reshape(1, d))


def _norm_matmul_body(x_ref, g_ref, w_ref, o_ref, xn_ref, *, out_scale, scaled_tiles, head_major):
    j = pl.program_id(1)

    @pl.when(j == 0)
    def _():
        xn_ref[...] = _rms_normed(x_ref[...], g_ref[...], RMS_EPS).astype(BF16)

    acc = _dot(xn_ref[...], w_ref[...])
    if scaled_tiles:
        acc = acc * jnp.where(j < scaled_tiles, out_scale, 1.0)
    if head_major:
        for hh in range(o_ref.shape[0]):
            o_ref[hh] = acc[:, hh * HEAD_DIM:(hh + 1) * HEAD_DIM].astype(o_ref.dtype)
    else:
        o_ref[...] = acc.astype(o_ref.dtype)


def _norm_matmul(x2d, gain, w, out_dtype, *, tm, tn, out_scale=1.0, scaled_cols=0, head_major=False):
    m, d = x2d.shape
    n = w.shape[1]
    assert scaled_cols % tn == 0
    if head_major:
        out_spec = pl.BlockSpec((tn // HEAD_DIM, tm, HEAD_DIM), lambda i, j: (j, i, 0))
        out_shape = jax.ShapeDtypeStruct((n // HEAD_DIM, m, HEAD_DIM), out_dtype)
    else:
        out_spec = pl.BlockSpec((tm, tn), lambda i, j: (i, j))
        out_shape = jax.ShapeDtypeStruct((m, n), out_dtype)
    return pl.pallas_call(
        functools.partial(_norm_matmul_body, out_scale=out_scale, scaled_tiles=scaled_cols // tn,
                          head_major=head_major),
        grid=(m // tm, n // tn),
        in_specs=[
            pl.BlockSpec((tm, d), lambda i, j: (i, 0)),
            pl.BlockSpec((1, d), lambda i, j: (0, 0)),
            pl.BlockSpec((d, tn), lambda i, j: (0, j)),
        ],
        out_specs=out_spec,
        out_shape=out_shape,
        scratch_shapes=[pltpu.VMEM((tm, d), BF16)],
        compiler_params=_params(("parallel", "arbitrary"), fusible_inputs=[False, False, True]),
        name="norm_matmul",
    )(x2d, gain.reshape(1, d), w)


def _matmul_res_body(a_ref, w_ref, x_ref, o_ref):
    o_ref[...] = x_ref[...] + _dot(a_ref[...], w_ref[...])


def _matmul_res(a2d, w, x2d, *, tm):
    m, k = a2d.shape
    d = w.shape[1]
    return pl.pallas_call(
        _matmul_res_body,
        grid=(m // tm,),
        in_specs=[
            pl.BlockSpec((tm, k), lambda i: (i, 0)),
            pl.BlockSpec((k, d), lambda i: (0, 0), pipeline_mode=pl.Buffered(1)),
            pl.BlockSpec((tm, d), lambda i: (i, 0)),
        ],
        out_specs=pl.BlockSpec((tm, d), lambda i: (i, 0)),
        out_shape=jax.ShapeDtypeStruct((m, d), F32),
        compiler_params=_params(("parallel",)),
        name="matmul_res",
    )(a2d, w, x2d)


def _bmm(a, b):
    return lax.dot_general(a, b, (((2,), (1,)), ((0,), (0,))), preferred_element_type=F32)


def _bmm_nt(a, b):
    return lax.dot_general(a, b, (((2,), (2,)), ((0,), (0,))), preferred_element_type=F32)


def _residue_rows(c, sub_len, dilation):
    return pl.ds(c, sub_len) if dilation == 1 else pl.ds(c, sub_len, stride=dilation)


def _dilated_attn_body(slopes_ref, qkv_ref, o_ref, og_ref, lse_ref, stage_ref, *, seq, heads_per_step):
    n_groups = len(DILATION_PATTERNS)

    qi = lax.broadcasted_iota(jnp.int32, (N_BACK, N_BACK), 0)
    ki = lax.broadcasted_iota(jnp.int32, (N_BACK, N_BACK), 1)
    steps_cur = (qi - ki).astype(F32)
    steps_prev = steps_cur + float(N_BACK)

    for hh in range(heads_per_step):
        slope = slopes_ref[pl.program_id(1) * heads_per_step + hh]
        outs, lses = [], []
        for g, (window, dilation) in enumerate(DILATION_PATTERNS):
            q_ref, k_ref, v_ref = (qkv_ref.at[which, g, hh] for which in range(3))
            sub_len = seq // dilation
            n_blk = sub_len // N_BACK

            def blocks(ref, which):
                if dilation > SINGLE_OP_STRIDE:
                    outer = dilation // SINGLE_OP_STRIDE
                    for r in range(SINGLE_OP_STRIDE):
                        stage_ref[which, r] = ref[pl.ds(r, seq // SINGLE_OP_STRIDE, stride=SINGLE_OP_STRIDE), :]
                    parts = [stage_ref[which, c % SINGLE_OP_STRIDE,
                                       pl.ds(c // SINGLE_OP_STRIDE, sub_len, stride=outer), :]
                             .reshape(n_blk, N_BACK, HEAD_DIM) for c in range(dilation)]
                else:
                    parts = [ref[_residue_rows(c, sub_len, dilation), :].reshape(n_blk, N_BACK, HEAD_DIM)
                             for c in range(dilation)]
                return parts[0] if dilation == 1 else jnp.concatenate(parts, axis=0)

            def later(a):
                return jnp.concatenate([a[c * n_blk + 1:(c + 1) * n_blk] for c in range(dilation)], axis=0)

            def earlier(a):
                return jnp.concatenate([a[c * n_blk:(c + 1) * n_blk - 1] for c in range(dilation)], axis=0)

            def pad_first(a, fill):
                first = jnp.full((1,) + a.shape[1:], fill, a.dtype)
                parts = []
                for c in range(dilation):
                    parts += [first, a[c * (n_blk - 1):(c + 1) * (n_blk - 1)]]
                return jnp.concatenate(parts, axis=0)

            q = blocks(q_ref, 0).astype(BF16)
            k = blocks(k_ref, 1).astype(BF16)
            v = blocks(v_ref, 2).astype(BF16)
            v_ones = jnp.concatenate([v, jnp.ones_like(v)], axis=-1)

            coef = -slope * (float(dilation) * LOG2_E)
            s_cur = _bmm_nt(q, k) + jnp.where(ki <= qi, coef * steps_cur, MASK_VALUE)
            m = jnp.max(s_cur, axis=-1, keepdims=True)
            if n_blk > 1:
                s_prev = _bmm_nt(later(q), earlier(k)) + jnp.where(ki >= qi, coef * steps_prev, MASK_VALUE)
                m = jnp.maximum(m, pad_first(jnp.max(s_prev, axis=-1, keepdims=True), MASK_VALUE))
                p_prev = jnp.exp2(s_prev - later(m)).astype(BF16)
            ol = _bmm(jnp.exp2(s_cur - m).astype(BF16), v_ones)
            if n_blk > 1:
                ol = ol + pad_first(_bmm(p_prev, earlier(v_ones)), 0.0)
            denom = ol[..., HEAD_DIM:]
            o = ol[..., :HEAD_DIM] / denom
            lse = m + jnp.log2(denom)
            if dilation == 1:
                outs.append(o.reshape(seq, HEAD_DIM))
                lses.append(lse.reshape(seq, HEAD_DIM))
            else:
                for which, (dst_ref, val) in enumerate(((og_ref, o), (lse_ref, lse))):
                    if dilation > SINGLE_OP_STRIDE:
                        outer = dilation // SINGLE_OP_STRIDE
                        for c in range(dilation):
                            stage_ref[which, c % SINGLE_OP_STRIDE,
                                      pl.ds(c // SINGLE_OP_STRIDE, sub_len, stride=outer), :] = (
                                val[c * n_blk:(c + 1) * n_blk].reshape(sub_len, HEAD_DIM))
                        for r in range(SINGLE_OP_STRIDE):
                            dst_ref[g - 1, pl.ds(r, seq // SINGLE_OP_STRIDE, stride=SINGLE_OP_STRIDE), :] = (
                                stage_ref[which, r])
                    else:
                        for c in range(dilation):
                            dst_ref[g - 1, _residue_rows(c, sub_len, dilation), :] = (
                                val[c * n_blk:(c + 1) * n_blk].reshape(sub_len, HEAD_DIM))
                outs.append(og_ref[g - 1])
                lses.append(lse_ref[g - 1])

        top = functools.reduce(jnp.maximum, lses)
        ws = [jnp.exp2(l - top) for l in lses]
        num = sum(w * o for w, o in zip(ws, outs))
        o_ref[:, hh * HEAD_DIM:(hh + 1) * HEAD_DIM] = (num / sum(ws)).astype(o_ref.dtype)


def _dilated_attn(qkv, slopes, *, batch, n_heads):
    s = qkv.shape[1] // batch
    n_groups = len(DILATION_PATTERNS)
    hps = math.gcd(n_heads, MAX_HEADS_PER_STEP)

    qkv = qkv.reshape(3, n_groups, n_heads, batch * s, HEAD_DIM)
    in_specs = [pl.BlockSpec(memory_space=pltpu.SMEM),
                pl.BlockSpec((3, n_groups, hps, s, HEAD_DIM), lambda bi, h: (0, 0, h, bi, 0))]
    return pl.pallas_call(
        functools.partial(_dilated_attn_body, seq=s, heads_per_step=hps),
        grid=(batch, n_heads // hps),
        in_specs=in_specs,
        out_specs=pl.BlockSpec((None, s, hps * HEAD_DIM), lambda bi, h: (bi, 0, h)),
        out_shape=jax.ShapeDtypeStruct((batch, s, n_heads * HEAD_DIM), BF16),
        scratch_shapes=[pltpu.VMEM((n_groups - 1, s, HEAD_DIM), F32),
                        pltpu.VMEM((n_groups - 1, s, HEAD_DIM), F32),
                        pltpu.VMEM((3, SINGLE_OP_STRIDE, s // SINGLE_OP_STRIDE, HEAD_DIM), F32)],
        compiler_params=_params(("parallel", "parallel")),
        name="dilated_attn",
    )(slopes, qkv)


def _diff_attn_body(slopes_ref, lam_ref, q_ref, k_ref, v_ref, gain_ref, o_ref,
                    *, seq, tq, lambda_init, heads_per_step):
    vdim = 2 * HEAD_DIM

    lam = lam_ref[...]
    lam_full = (jnp.exp(jnp.sum(lam[0:1] * lam[1:2], axis=-1, keepdims=True))
                - jnp.exp(jnp.sum(lam[2:3] * lam[3:4], axis=-1, keepdims=True))
                + lambda_init)

    key_pos = lax.broadcasted_iota(jnp.int32, (1, seq), 1).astype(F32)
    qi = lax.broadcasted_iota(jnp.int32, (tq, tq), 0)
    ki = lax.broadcasted_iota(jnp.int32, (tq, tq), 1)
    causal = jnp.where(qi >= ki, 0.0, MASK_VALUE)

    for hh in range(heads_per_step):
        slope = slopes_ref[pl.program_id(1) * heads_per_step + hh] * LOG2_E
        c0 = hh * vdim
        key_bias = slope * key_pos

        for i in range(seq // tq):
            lo, hi = i * tq, (i + 1) * tq
            probs, denoms = [], []
            for cols in (slice(c0, c0 + HEAD_DIM), slice(c0 + HEAD_DIM, c0 + vdim)):
                q = q_ref[lo:hi, cols]
                tiles = []
                m = None
                for j in range(i + 1):
                    klo, khi = j * tq, (j + 1) * tq
                    bias = key_bias[:, klo:khi] + causal if j == i else key_bias[:, klo:khi]
                    s = _dot_nt(q, k_ref[klo:khi, cols]) + bias
                    tile_max = jnp.max(s, axis=-1, keepdims=True)
                    m = tile_max if m is None else jnp.maximum(m, tile_max)
                    tiles.append(s)
                l = None
                ps = []
                for s in tiles:
                    p = jnp.exp2(s - m)
                    tile_sum = jnp.sum(p, axis=-1, keepdims=True)
                    l = tile_sum if l is None else l + tile_sum
                    ps.append(p.astype(BF16))
                probs.append(ps[0] if len(ps) == 1 else jnp.concatenate(ps, axis=1))
                denoms.append(l)
            pv = _dot(jnp.concatenate(probs, axis=0), v_ref[0:hi, c0:c0 + vdim])
            o = pv[:tq] / denoms[0] - lam_full * (pv[tq:] / denoms[1])
            o = _rms_normed(o, gain_ref[...], SUBLN_EPS) * (1.0 - lambda_init)
            o_ref[lo:hi, c0:c0 + vdim] = o.astype(o_ref.dtype)


def _diff_attn(q, kv, slopes, lam, subln_gain, *, n_heads, lambda_init, tq):
    b, s, _ = q.shape
    vdim = 2 * HEAD_DIM
    hps = math.gcd(n_heads, MAX_HEADS_PER_STEP)
    return pl.pallas_call(
        functools.partial(_diff_attn_body, seq=s, tq=tq, lambda_init=lambda_init,
                          heads_per_step=hps),
        grid=(b, n_heads // hps),
        in_specs=[
            pl.BlockSpec(memory_space=pltpu.SMEM),
            pl.BlockSpec((4, HEAD_DIM), lambda bi, h: (0, 0)),
            pl.BlockSpec((None, s, hps * vdim), lambda bi, h: (bi, 0, h)),
            pl.BlockSpec((None, s, hps * vdim), lambda bi, h: (bi, 0, h)),
            pl.BlockSpec((None, s, hps * vdim), lambda bi, h: (bi, 0, n_heads // hps + h)),
            pl.BlockSpec((1, vdim), lambda bi, h: (0, 0)),
        ],
        out_specs=pl.BlockSpec((None, s, hps * vdim), lambda bi, h: (bi, 0, h)),
        out_shape=jax.ShapeDtypeStruct((b, s, n_heads * vdim), BF16),
        compiler_params=_params(("parallel", "parallel")),
        name="diff_attn",
    )(slopes, lam, q, kv, kv, subln_gain.reshape(1, vdim))


def _alibi_slopes(n_heads):
    return 2.0 ** (-8.0 * jnp.arange(1, n_heads + 1, dtype=F32) / n_heads)


def _tile(total, preferred):
    return preferred if total % preferred == 0 else total


def kernel(x, norm_gains, ffn_w_in, ffn_w_out, a_w_qkv, a_w_o, kv_norm_gain, b_w_kv, b_w_q, b_lambda,
           b_subln_gain, b_w_o, final_norm_gain):
    b, s, d = x.shape
    m = b * s
    depth = norm_gains.shape[0]
    n_a_layers = a_w_qkv.shape[0]
    d_ff = ffn_w_out.shape[2]
    heads_a = a_w_o.shape[1] // HEAD_DIM
    heads_b = b_w_o.shape[1] // (2 * HEAD_DIM)

    tm_proj = _tile(m, 1024)
    tm_ffn = _tile(m, 1024)
    tf = 512
    d_ff_pad = -(-d_ff // tf) * tf

    slopes_a = _alibi_slopes(heads_a)
    slopes_b = _alibi_slopes(heads_b)

    w_in = _pack_w_in(ffn_w_in.reshape(2 * depth, d, 2 * d_ff), d_ff_pad, tr=_tile(d, 128))
    w_out = _pack_w_out(ffn_w_out.reshape(2 * depth, d_ff, d), d_ff_pad, tr=tf)

    def ffn(xs, layer, idx, final_gain=None):
        return _ffn(xs, norm_gains[layer, 0 if idx == 0 else 2], w_in, w_out, final_gain,
                    layer=layer, idx=idx, tm=tm_ffn, tf=tf)

    xs = x.reshape(m, d)
    kv = None
    for layer in range(depth):
        xs = ffn(xs, layer, 0)
        if layer < n_a_layers:
            q_cols = a_w_qkv.shape[2] // 3
            qkv = _norm_matmul(xs, norm_gains[layer, 1], a_w_qkv[layer].astype(BF16), F32,
                               tm=tm_proj, tn=_tile(q_cols, 1536),
                               out_scale=SCORE_SCALE_LOG2, scaled_cols=q_cols, head_major=True)
            merged = _dilated_attn(qkv, slopes_a, batch=b, n_heads=heads_a)
            xs = _matmul_res(merged.reshape(m, -1), a_w_o[layer].astype(BF16), xs, tm=tm_proj)
        else:
            j = layer - n_a_layers
            q = _norm_matmul(xs, norm_gains[layer, 1], b_w_q[j].astype(BF16), BF16,
                             tm=tm_proj, tn=_tile(b_w_q.shape[2], 2048),
                             out_scale=SCORE_SCALE_LOG2, scaled_cols=b_w_q.shape[2])
            lambda_init = 0.8 - 0.6 * math.exp(-0.3 * layer)
            o = _diff_attn(q.reshape(b, s, -1), kv, slopes_b, b_lambda[j], b_subln_gain[j],
                           n_heads=heads_b, lambda_init=lambda_init, tq=_tile(s, 256))
            xs = _matmul_res(o.reshape(m, -1), b_w_o[j].astype(BF16), xs, tm=tm_proj)
        last = layer == depth - 1
        xs = ffn(xs, layer, 1, final_norm_gain if last else None)
        if layer == n_a_layers - 1:
            kv = _norm_matmul(xs, kv_norm_gain, b_w_kv.astype(BF16), BF16,
                              tm=tm_proj, tn=_tile(b_w_kv.shape[1], 2048)).reshape(b, s, -1)
    return xs.reshape(b, s, d)
```
